```python
import jax, jax.numpy as jnp
from jax import lax
import numpy as np

D_MODEL = 1024
BATCH = 16
SEQ = 2048
DEPTH = 4
DEC_BATCH = 32
DEC_SEQ = 16
PAST_LEN = 4096

CHUNK = 64
N_MIXERS = 2
N_RWKV = (DEPTH + 1) // 2
N_ATT = DEPTH // 2
HEAD_DIM = 64
N_HEADS = D_MODEL // HEAD_DIM
RWKV_HEAD = 64
RWKV_HEADS = D_MODEL // RWKV_HEAD
DECAY_LORA = 64
ICLR_LORA = 64
VRES_LORA = 32
GATE_LORA = 160
LEFT_CHUNKS = 8
ATT_WINDOW = LEFT_CHUNKS * CHUNK
REL_MAX_PAST = 256
REL_SIZE = REL_MAX_PAST + CHUNK
D_FF = -(-8 * D_MODEL // (3 * 256)) * 256
RMS_EPS = 1e-6
LNX_EPS = 64e-5
NEG_INF = -1e30

kernel_name = "rwkv7_chunkband_hybrid_stream_step"


def rms_norm(x, g):
    x32 = x.astype(jnp.float32)
    y = x32 * lax.rsqrt(jnp.mean(x32 * x32, axis=-1, keepdims=True) + RMS_EPS)
    return (y * g.astype(jnp.float32)).astype(x.dtype)


def swiglu_ffn(x, w_gu, w_down):
    gate, up = jnp.split(x @ w_gu, 2, axis=-1)
    return (jax.nn.silu(gate) * up) @ w_down


def rwkv7_time_mix(xn, shift, s0, v_first, p, vres):
    (mu, w_rkv, w_out, w0, w1, w2, a0, a1, a2, g1, g2, k_k, k_a, r_k, lnx_w, lnx_b) = p
    B, T, D = xn.shape
    f32 = jnp.float32
    x = xn.astype(f32)
    x_prev = jnp.concatenate([shift.astype(f32)[:, None], x[:, :-1]], axis=1)
    xx = x_prev - x
    mixed = x[None] + xx[None] * mu.astype(f32)[:, None, None, :]
    rkv = jnp.einsum('jbtd,jde->jbte', mixed[:3], w_rkv.astype(f32))
    r, k, v = rkv[0], rkv[1], rkv[2]
    w_log = -jax.nn.softplus(-(w0 + jnp.tanh(mixed[3] @ w1) @ w2)) - 0.5
    decay = jnp.exp(-jnp.exp(w_log))
    a = jax.nn.sigmoid(a0 + (mixed[4] @ a1) @ a2)
    g = jax.nn.sigmoid(mixed[5] @ g1) @ g2
    if vres is None:
        v_first = v
    else:
        v0, v1, v2 = vres
        v = v + (v_first - v) * jax.nn.sigmoid(v0 + (mixed[2] @ v1) @ v2)
    heads = lambda t: t.reshape(B, T, RWKV_HEADS, RWKV_HEAD)
    kk = heads(k * k_k)
    kk = kk / jnp.maximum(jnp.sqrt(jnp.sum(kk * kk, axis=-1, keepdims=True)), 1e-12)
    k = k * (1.0 + (a - 1.0) * k_a)
    r_h, k_h, v_h, w_h, a_h = heads(r), heads(k), heads(v), heads(decay), heads(a)
    b_h = kk * a_h

    def step(S, inp):
        r_t, w_t, k_t, v_t, na_t, b_t = inp
        sa = jnp.einsum('bhvk,bhk->bhv', S, na_t)
        S = S * w_t[:, :, None, :] + sa[..., None] * b_t[:, :, None, :] + v_t[..., None] * k_t[:, :, None, :]
        return S, jnp.einsum('bhvk,bhk->bhv', S, r_t)

    xs = (jnp.moveaxis(r_h, 1, 0), jnp.moveaxis(w_h, 1, 0), jnp.moveaxis(k_h, 1, 0),
          jnp.moveaxis(v_h, 1, 0), jnp.moveaxis(-kk, 1, 0), jnp.moveaxis(b_h, 1, 0))
    s_final, out = lax.scan(step, s0.astype(f32), xs)
    out = jnp.moveaxis(out, 0, 1)
    mean = jnp.mean(out, axis=-1, keepdims=True)
    var = jnp.mean(jnp.square(out - mean), axis=-1, keepdims=True)
    out = ((out - mean) * lax.rsqrt(var + LNX_EPS)).reshape(B, T, D) * lnx_w + lnx_b
    bonus = jnp.sum(r_h * k_h * r_k, axis=-1, keepdims=True) * v_h
    out = (out + bonus.reshape(B, T, D)) * g
    y = (out @ w_out.astype(f32)).astype(xn.dtype)
    return y, xn[:, -1], s_final.astype(xn.dtype), v_first


def split_qkv(xn, w_qkv):
    B, T, _ = xn.shape
    q, k, v = jnp.split(xn @ w_qkv, 3, axis=-1)
    shp = (B, T, N_HEADS, HEAD_DIM)
    return q.reshape(shp), k.reshape(shp), v.reshape(shp)


def band_attention(q, k, v, qpos, kpos, rel_bias):
    s = jnp.einsum('bqhd,bkhd->bhqk', q, k).astype(jnp.float32) * (HEAD_DIM ** -0.5)
    rel = jnp.clip(qpos[:, None] - kpos[None, :], -(CHUNK - 1), REL_MAX_PAST) + (CHUNK - 1)
    s = s + rel_bias.astype(jnp.float32)[:, rel][None]
    qc, kc = qpos // CHUNK, kpos // CHUNK
    allowed = (kpos[None, :] >= 0) & (kc[None, :] <= qc[:, None]) & (kc[None, :] >= qc[:, None] - LEFT_CHUNKS)
    s = jnp.where(allowed[None, None], s, NEG_INF)
    p = jax.nn.softmax(s, axis=-1)
    return jnp.einsum('bhqk,bkhd->bqhd', p.astype(v.dtype), v)


def chunk_attention_prompt(xn, w_qkv, w_out, rel_bias):
    B, S, D = xn.shape
    q, k, v = split_qkv(xn, w_qkv)
    pad = ATT_WINDOW
    band = pad + CHUNK
    kp = jnp.pad(k, ((0, 0), (pad, 0), (0, 0), (0, 0)))
    vp = jnp.pad(v, ((0, 0), (pad, 0), (0, 0), (0, 0)))

    def one_chunk(c):
        start = c * CHUNK
        qc = lax.dynamic_slice_in_dim(q, start, CHUNK, axis=1)
        kc = lax.dynamic_slice_in_dim(kp, start, band, axis=1)
        vc = lax.dynamic_slice_in_dim(vp, start, band, axis=1)
        qpos = start + jnp.arange(CHUNK, dtype=jnp.int32)
        kpos = start - pad + jnp.arange(band, dtype=jnp.int32)
        return band_attention(qc, kc, vc, qpos, kpos, rel_bias)

    out = lax.map(one_chunk, jnp.arange(S // CHUNK, dtype=jnp.int32))
    out = jnp.moveaxis(out, 0, 1).reshape(B, S, D)
    rows = min(ATT_WINDOW, S)
    return out @ w_out, k[:, S - rows:], v[:, S - rows:]


def chunk_attention_sample(xn, ck, cv, w_qkv, w_out, rel_bias):
    B, T, D = xn.shape
    q, k, v = split_qkv(xn, w_qkv)
    W = ck.shape[1]
    keys = jnp.concatenate([ck.astype(k.dtype), k], axis=1)
    vals = jnp.concatenate([cv.astype(v.dtype), v], axis=1)
    qpos = PAST_LEN + jnp.arange(T, dtype=jnp.int32)
    kpos = PAST_LEN - W + jnp.arange(W + T, dtype=jnp.int32)
    out = band_attention(q, keys, vals, qpos, kpos, rel_bias).reshape(B, T, D)
    return out @ w_out, k, v


def setup_inputs(seed: int = 0) -> dict:
    key = jax.random.key(seed)
    ks = iter(jax.random.split(key, 48))
    f32 = jnp.float32

    def nrm(shape, scale):
        return scale * jax.random.normal(next(ks), shape, f32)

    def unif(shape, lo, hi):
        return jax.random.uniform(next(ks), shape, f32, minval=lo, maxval=hi)

    D, H, N = D_MODEL, RWKV_HEADS, RWKV_HEAD
    W = min(ATT_WINDOW, PAST_LEN)
    nv = max(N_RWKV - 1, 0)
    return {
        "x_prompt": nrm((BATCH, SEQ, D), 1.0),
        "x_sample": nrm((DEC_BATCH, DEC_SEQ, D), 1.0),
        "state_wkv": nrm((N_RWKV, DEC_BATCH, H, N, N), 0.2),
        "state_shift": nrm((N_RWKV, DEC_BATCH, D), 1.0),
        "cache_k": nrm((N_ATT, DEC_BATCH, W, N_HEADS, HEAD_DIM), 1.0),
        "cache_v": nrm((N_ATT, DEC_BATCH, W, N_HEADS, HEAD_DIM), 1.0),
        "norm_mix": 1.0 + nrm((DEPTH, D), 0.05),
        "norm_ffn": 1.0 + nrm((DEPTH, D), 0.05),
        "norm_final": 1.0 + nrm((D,), 0.05),
        "rwkv_mu": unif((N_RWKV, 6, D), 0.0, 1.0),
        "rwkv_w_rkv": nrm((N_RWKV, 3, D, D), D ** -0.5),
        "rwkv_w_out": nrm((N_RWKV, D, D), D ** -0.5),
        "rwkv_decay_w0": unif((N_RWKV, D), -6.0, 0.0),
        "rwkv_decay_w1": nrm((N_RWKV, D, DECAY_LORA), D ** -0.5),
        "rwkv_decay_w2": nrm((N_RWKV, DECAY_LORA, D), 0.1 * DECAY_LORA ** -0.5),
        "rwkv_iclr_a0": nrm((N_RWKV, D), 0.5),
        "rwkv_iclr_a1": nrm((N_RWKV, D, ICLR_LORA), D ** -0.5),
        "rwkv_iclr_a2": nrm((N_RWKV, ICLR_LORA, D), 0.1 * ICLR_LORA ** -0.5),
        "rwkv_vres_v0": nrm((nv, D), 0.5),
        "rwkv_vres_v1": nrm((nv, D, VRES_LORA), D ** -0.5),
        "rwkv_vres_v2": nrm((nv, VRES_LORA, D), 0.1 * VRES_LORA ** -0.5),
        "rwkv_gate_g1": nrm((N_RWKV, D, GATE_LORA), D ** -0.5),
        "rwkv_gate_g2": nrm((N_RWKV, GATE_LORA, D), GATE_LORA ** -0.5),
        "rwkv_k_k": 0.85 + nrm((N_RWKV, D), 0.05),
        "rwkv_k_a": 1.0 + nrm((N_RWKV, D), 0.05),
        "rwkv_r_k": nrm((N_RWKV, H, N), 0.1),
        "rwkv_lnx_w": 1.0 + nrm((N_RWKV, D), 0.05),
        "rwkv_lnx_b": nrm((N_RWKV, D), 0.02),
        "attn_w_qkv": nrm((N_ATT, D, 3 * D), D ** -0.5),
        "attn_w_out": nrm((N_ATT, D, D), D ** -0.5),
        "attn_rel_bias": nrm((N_ATT, N_HEADS, REL_SIZE), 0.5),
        "ffn_w_gu": nrm((DEPTH, D, 2 * D_FF), D ** -0.5),
        "ffn_w_down": nrm((DEPTH, D_FF, D), D_FF ** -0.5),
    }


def reference(x_prompt, x_sample, state_wkv, state_shift, cache_k, cache_v,
              norm_mix, norm_ffn, norm_final,
              rwkv_mu, rwkv_w_rkv, rwkv_w_out, rwkv_decay_w0, rwkv_decay_w1, rwkv_decay_w2,
              rwkv_iclr_a0, rwkv_iclr_a1, rwkv_iclr_a2, rwkv_vres_v0, rwkv_vres_v1, rwkv_vres_v2,
              rwkv_gate_g1, rwkv_gate_g2, rwkv_k_k, rwkv_k_a, rwkv_r_k, rwkv_lnx_w, rwkv_lnx_b,
              attn_w_qkv, attn_w_out, attn_rel_bias, ffn_w_gu, ffn_w_down):
    B = x_prompt.shape[0]
    hp, hs = x_prompt, x_sample
    vf_p = vf_s = None
    wkv_p, shift_p, k_p, v_p = [], [], [], []
    wkv_s, shift_s, k_s, v_s = [], [], [], []
    for layer in range(DEPTH):
        xp = rms_norm(hp, norm_mix[layer])
        xs = rms_norm(hs, norm_mix[layer])
        j = layer // N_MIXERS
        if layer % N_MIXERS == 0:
            p = (rwkv_mu[j], rwkv_w_rkv[j], rwkv_w_out[j], rwkv_decay_w0[j], rwkv_decay_w1[j],
                 rwkv_decay_w2[j], rwkv_iclr_a0[j], rwkv_iclr_a1[j], rwkv_iclr_a2[j],
                 rwkv_gate_g1[j], rwkv_gate_g2[j], rwkv_k_k[j], rwkv_k_a[j], rwkv_r_k[j],
                 rwkv_lnx_w[j], rwkv_lnx_b[j])
            vres = None if j == 0 else (rwkv_vres_v0[j - 1], rwkv_vres_v1[j - 1], rwkv_vres_v2[j - 1])
            zero_shift = jnp.zeros((B, D_MODEL), xp.dtype)
            zero_state = jnp.zeros((B, RWKV_HEADS, RWKV_HEAD, RWKV_HEAD), jnp.float32)
            yp, sh_p, st_p, vf_p = rwkv7_time_mix(xp, zero_shift, zero_state, vf_p, p, vres)
            ys, sh_s, st_s, vf_s = rwkv7_time_mix(xs, state_shift[j], state_wkv[j], vf_s, p, vres)
            wkv_p.append(st_p); shift_p.append(sh_p)
            wkv_s.append(st_s); shift_s.append(sh_s)
        else:
            yp, kp_new, vp_new = chunk_attention_prompt(xp, attn_w_qkv[j], attn_w_out[j], attn_rel_bias[j])
            ys, ks_new, vs_new = chunk_attention_sample(xs, cache_k[j], cache_v[j], attn_w_qkv[j],
                                                        attn_w_out[j], attn_rel_bias[j])
            k_p.append(kp_new); v_p.append(vp_new)
            k_s.append(ks_new); v_s.append(vs_new)
        hp = hp + yp
        hs = hs + ys
        hp = hp + swiglu_ffn(rms_norm(hp, norm_ffn[layer]), ffn_w_gu[layer], ffn_w_down[layer])
        hs = hs + swiglu_ffn(rms_norm(hs, norm_ffn[layer]), ffn_w_gu[layer], ffn_w_down[layer])
    y_prompt = rms_norm(hp, norm_final)
    y_sample = rms_norm(hs, norm_final)
    return (y_prompt, y_sample,
            jnp.stack(wkv_p), jnp.stack(shift_p), jnp.stack(k_p), jnp.stack(v_p),
            jnp.stack(wkv_s), jnp.stack(shift_s), jnp.stack(k_s), jnp.stack(v_s))
```

```python
import functools

import jax
import jax.numpy as jnp
from jax import lax
from jax.experimental import pallas as pl
from jax.experimental.pallas import tpu as pltpu

F32 = jnp.float32
BF16 = jnp.bfloat16

HEAD = 64
CHUNK = 64
LEFT_CHUNKS = 8
ATT_WINDOW = LEFT_CHUNKS * CHUNK
REL_MAX_PAST = 256
PAST_LEN = 4096
RMS_EPS = 1e-6
LNX_EPS = 64e-5
NEG_INF = -1e30
WKV_CHUNK = 64
SEG_LANES = 128
VMEM_LIMIT = 56 * 1024 * 1024


def _dot(a, b):
    return jnp.dot(a, b, preferred_element_type=F32)


def _dot_nt(a, b):
    return lax.dot_general(a, b, (((1,), (1,)), ((), ())), preferred_element_type=F32)


def _dot_tn(a, b):
    return lax.dot_general(a, b, (((0,), (0,)), ((), ())), preferred_element_type=F32)


def _rms(x, g):
    return x * lax.rsqrt(jnp.mean(x * x, axis=-1, keepdims=True) + RMS_EPS) * g


def _const_spec(shape):
    nd = len(shape)
    return pl.BlockSpec(shape, lambda *_: (0,) * nd, pipeline_mode=pl.Buffered(1))


def _seg_sum(x, seg_ref):
    return _dot(x.astype(BF16), seg_ref[...])


def _seg_expand(s, exp_ref):
    hi = s.astype(BF16)
    lo = (s - hi.astype(F32)).astype(BF16)
    return _dot(jnp.concatenate([hi, lo], axis=-1), exp_ref[...])


def _tmix_proj_kernel(*refs, rows_mode, has_vres, seq_blocks):
    it = iter(refs)
    h_ref = next(it)
    bnd_ref = next(it)
    shift_ref = None if rows_mode else next(it)
    vfirst_ref = next(it) if has_vres else None
    gn_ref = next(it)
    mu_ref = next(it)
    wrkv_ref = next(it)
    w0_ref, w1_ref, w2_ref = next(it), next(it), next(it)
    a0_ref, a1_ref, a2_ref = next(it), next(it), next(it)
    if has_vres:
        v0_ref, v1_ref, v2_ref = next(it), next(it), next(it)
    g1_ref, g2_ref = next(it), next(it)
    kk_ref, ka_ref, rk_ref = next(it), next(it), next(it)
    seg_ref, exp_ref = next(it), next(it)
    (r_out, k_out, v_out, lw_out, kk_out, bb_out, g_out, bonus_out, xl_out) = it

    gn = gn_ref[...]
    x = _rms(h_ref[0], gn)
    tm = x.shape[0]
    row = lax.broadcasted_iota(jnp.int32, x.shape, 0)
    rolled = pltpu.roll(x, 1, 0)
    if rows_mode:
        seq_len = seq_blocks
        x_prev = jnp.where(row % seq_len == 0, bnd_ref[0], rolled)
        xl_out[0] = x
    else:
        halo = bnd_ref[0]
        prev_row = _rms(halo[7:8, :], gn)
        first = pl.program_id(1) % seq_blocks == 0
        prev_row = jnp.where(first, shift_ref[0], prev_row)
        x_prev = jnp.where(row == 0, prev_row, rolled)
        xl_out[0] = x[tm - 1:tm, :]
    xx = x_prev - x

    def mixed(j):
        return (x + xx * mu_ref[j:j + 1, :]).astype(BF16)

    m_v = mixed(2)
    r = _dot(mixed(0), wrkv_ref[0])
    k = _dot(mixed(1), wrkv_ref[1])
    v = _dot(m_v, wrkv_ref[2])

    w_pre = w0_ref[...] + _dot(jnp.tanh(_dot(mixed(3), w1_ref[...])).astype(BF16), w2_ref[...])
    lw = jax.nn.sigmoid(w_pre) * (-0.6065306597126334)
    a = jax.nn.sigmoid(a0_ref[...] + _dot(_dot(mixed(4), a1_ref[...]).astype(BF16), a2_ref[...]))
    g = _dot(jax.nn.sigmoid(_dot(mixed(5), g1_ref[...])).astype(BF16), g2_ref[...])
    if has_vres:
        gate = jax.nn.sigmoid(v0_ref[...] + _dot(_dot(m_v, v1_ref[...]).astype(BF16), v2_ref[...]))
        v = v + (vfirst_ref[0] - v) * gate

    kk = k * kk_ref[...]
    norm = jnp.maximum(jnp.sqrt(_seg_sum(kk * kk, seg_ref)), 1e-12)
    kk = kk * _seg_expand(1.0 / norm, exp_ref)
    k = k * (1.0 + (a - 1.0) * ka_ref[...])
    bonus = _seg_expand(_seg_sum(r * k * rk_ref[...], seg_ref), exp_ref) * v

    r_out[0] = r
    k_out[0] = k
    v_out[0] = v
    lw_out[0] = lw
    kk_out[0] = kk
    bb_out[0] = kk * a
    g_out[0] = g
    bonus_out[0] = bonus


def _tmix_proj(h, shift, vfirst, gn, p, seg, exp, *, tm):
    B, T, D = h.shape
    rows_mode = T < tm
    has_vres = vfirst is not None
    if rows_mode:
        nseq = tm // T
        assert (B * T) % tm == 0
        hb = h.reshape(B * T // tm, tm, D)
        bnd = jnp.zeros((B, T, D), F32).at[:, 0, :].set(shift).reshape(hb.shape)
        grid = (hb.shape[0], 1)
        blk = pl.BlockSpec((1, tm, D), lambda b, i: (b, 0, 0))
        in_arrays = [hb, bnd]
        in_specs = [blk, blk]
        seq_blocks = T
        if has_vres:
            in_arrays.append(vfirst.reshape(hb.shape))
            in_specs.append(blk)
        xl_shape = jax.ShapeDtypeStruct(hb.shape, F32)
        xl_spec = blk
        out_shape_main = hb.shape
    else:
        assert T % tm == 0 and tm % 8 == 0
        nblk = T // tm
        grid = (B, nblk)
        blk = pl.BlockSpec((1, tm, D), lambda b, i: (b, i, 0))
        halo = pl.BlockSpec((1, 8, D), lambda b, i: (b, jnp.maximum(i * (tm // 8) - 1, 0), 0))
        in_arrays = [h, h, shift.reshape(B, 1, D)]
        in_specs = [blk, halo, pl.BlockSpec((1, 1, D), lambda b, i: (b, 0, 0))]
        seq_blocks = nblk
        if has_vres:
            in_arrays.append(vfirst)
            in_specs.append(blk)
        xl_shape = jax.ShapeDtypeStruct((B, 1, D), F32)
        xl_spec = pl.BlockSpec((1, 1, D), lambda b, i: (b, 0, 0))
        out_shape_main = h.shape

    weights = [gn, p["mu"], p["w_rkv"], p["w0"], p["w1"], p["w2"], p["a0"], p["a1"], p["a2"]]
    if has_vres:
        weights += [p["v0"], p["v1"], p["v2"]]
    weights += [p["g1"], p["g2"], p["k_k"], p["k_a"], p["r_k"], seg, exp]
    in_arrays += weights
    in_specs += [_const_spec(w.shape) for w in weights]

    main = jax.ShapeDtypeStruct(out_shape_main, F32)
    outs = pl.pallas_call(
        functools.partial(_tmix_proj_kernel, rows_mode=rows_mode, has_vres=has_vres, seq_blocks=seq_blocks),
        grid=grid,
        in_specs=in_specs,
        out_specs=[blk] * 8 + [xl_spec],
        out_shape=[main] * 8 + [xl_shape],
        compiler_params=pltpu.CompilerParams(
            dimension_semantics=("parallel", "arbitrary"), vmem_limit_bytes=VMEM_LIMIT),
        name="tmix_proj",
    )(*in_arrays)
    main_outs = [o.reshape(B, T, D) for o in outs[:8]]
    if rows_mode:
        x_last = outs[8].reshape(B, T, D)[:, T - 1, :]
    else:
        x_last = outs[8].reshape(B, D)
    return main_outs, x_last


def _wkv_chunk_kernel(r_ref, k_ref, v_ref, lw_ref, kk_ref, bb_ref, s0_ref, o_ref, s_out_ref, s_scr, *, heads):
    c = pl.program_id(1)
    C = r_ref.shape[1]

    @pl.when(c == 0)
    def _():
        s_scr[...] = s0_ref[0]

    lw = lw_ref[0]
    ti = lax.broadcasted_iota(jnp.int32, (C, C), 0)
    si = lax.broadcasted_iota(jnp.int32, (C, C), 1)
    tril_incl = ti >= si
    tril_strict = ti > si
    cs = jnp.dot(tril_incl.astype(F32), lw, preferred_element_type=F32, precision=lax.Precision.HIGHEST)
    p_incl = jnp.exp(cs)
    p_inv = jnp.exp(-cs)
    p_prev = jnp.exp(cs - lw)
    p_last = p_incl[C - 1:C, :]

    rt = (r_ref[0] * p_incl).astype(BF16)
    nt = (kk_ref[0] * (-p_prev)).astype(BF16)
    bt = (bb_ref[0] * p_inv).astype(BF16)
    kt = (k_ref[0] * p_inv).astype(BF16)
    vv = v_ref[0].astype(BF16)

    n_stage = C.bit_length() - 1
    for h in range(heads):
        sl = slice(h * HEAD, (h + 1) * HEAD)
        nr = jnp.concatenate([nt[:, sl], rt[:, sl]], axis=0)
        b_h, k_h, v_h = bt[:, sl], kt[:, sl], vv[:, sl]
        s_prev = s_scr[h]
        gb = _dot_nt(nr, b_h)
        gk = _dot_nt(nr, k_h)
        x0 = _dot_nt(nr, s_prev.astype(BF16))
        a_b = jnp.where(tril_strict, gb[:C], 0.0)
        a_k = jnp.where(tril_strict, gk[:C], 0.0)
        l_b = jnp.where(tril_incl, gb[C:], 0.0)
        l_k = jnp.where(tril_incl, gk[C:], 0.0)
        u = x0[:C] + _dot(a_k.astype(BF16), v_h)
        apow = a_b
        for i in range(n_stage):
            u = u + _dot(apow, u)
            if i + 1 < n_stage:
                apow = _dot(apow, apow)
        u16 = u.astype(BF16)
        o = x0[C:] + _dot(l_b.astype(BF16), u16) + _dot(l_k.astype(BF16), v_h)
        o_ref[0, :, sl] = o
        upd = _dot_tn(jnp.concatenate([u16, v_h], axis=0), jnp.concatenate([b_h, k_h], axis=0))
        s_scr[h] = (s_prev + upd) * p_last[:, sl]

    @pl.when(c == pl.num_programs(1) - 1)
    def _():
        s_out_ref[0] = s_scr[...]


def _wkv_chunked(r, k, v, lw, kk, bb, s0, *, chunk):
    B, T, D = r.shape
    heads = D // HEAD
    assert T % chunk == 0
    blk = pl.BlockSpec((1, chunk, D), lambda b, c: (b, c, 0))
    st = pl.BlockSpec((1, heads, HEAD, HEAD), lambda b, c: (b, 0, 0, 0))
    o, s_out = pl.pallas_call(
        functools.partial(_wkv_chunk_kernel, heads=heads),
        grid=(B, T // chunk),
        in_specs=[blk] * 6 + [st],
        out_specs=[blk, st],
        out_shape=[jax.ShapeDtypeStruct((B, T, D), F32), jax.ShapeDtypeStruct(s0.shape, F32)],
        scratch_shapes=[pltpu.VMEM((heads, HEAD, HEAD), F32)],
        compiler_params=pltpu.CompilerParams(
            dimension_semantics=("parallel", "arbitrary"), vmem_limit_bytes=VMEM_LIMIT),
        name="wkv_chunk",
    )(r, k, v, lw, kk, bb, s0)
    return o, s_out


def _qkv_kernel(h_ref, gn_ref, w_ref, q_out, kp_out, vp_out, kl_out, vl_out, *, npad, last_from):
    i = pl.program_id(1)
    D = h_ref.shape[2]

    @pl.when(i < npad)
    def _():
        kp_out[0] = jnp.zeros(kp_out.shape[1:], kp_out.dtype)
        vp_out[0] = jnp.zeros(vp_out.shape[1:], vp_out.dtype)

    @pl.when(i >= npad)
    def _():
        xn = _rms(h_ref[0], gn_ref[...]).astype(BF16)
        q = _dot(xn, w_ref[:, 0:D])
        k = _dot(xn, w_ref[:, D:2 * D])
        v = _dot(xn, w_ref[:, 2 * D:3 * D])
        q_out[0] = (q * (HEAD ** -0.5)).astype(BF16)
        kp_out[0] = k.astype(BF16)
        vp_out[0] = v.astype(BF16)

        @pl.when(i >= npad + last_from)
        def _():
            kl_out[0] = k
            vl_out[0] = v


def _qkv_proj(h, gn, w_qkv, *, tq, pad_rows, keep):
    B, T, D = h.shape
    assert T % tq == 0 and pad_rows % tq == 0
    npad = pad_rows // tq
    assert keep % tq == 0
    last_from = (T - keep) // tq
    blk_in = pl.BlockSpec((1, tq, D), lambda b, i: (b, jnp.maximum(i - npad, 0), 0))
    blk_pad = pl.BlockSpec((1, tq, D), lambda b, i: (b, i, 0))
    blk_last = pl.BlockSpec((1, tq, D), lambda b, i: (b, jnp.maximum(i - npad - last_from, 0), 0))
    return pl.pallas_call(
        functools.partial(_qkv_kernel, npad=npad, last_from=last_from),
        grid=(B, npad + T // tq),
        in_specs=[blk_in, _const_spec(gn.shape), _const_spec(w_qkv.shape)],
        out_specs=[blk_in, blk_pad, blk_pad, blk_last, blk_last],
        out_shape=[jax.ShapeDtypeStruct((B, T, D), BF16),
                   jax.ShapeDtypeStruct((B, pad_rows + T, D), BF16),
                   jax.ShapeDtypeStruct((B, pad_rows + T, D), BF16),
                   jax.ShapeDtypeStruct((B, keep, D), F32),
                   jax.ShapeDtypeStruct((B, keep, D), F32)],
        compiler_params=pltpu.CompilerParams(
            dimension_semantics=("parallel", "arbitrary"), vmem_limit_bytes=VMEM_LIMIT),
        name="qkv_proj",
    )(h, gn, w_qkv)


def _band_attn_kernel(q_ref, k_ref, v_ref, tbl_ref, o_ref, *, nk, windowed, pad_rows):
    mq = q_ref.shape[1]
    q2 = q_ref[0]
    if windowed:
        c = pl.program_id(2)
        start = pl.multiple_of(c * mq, mq)
        kb = k_ref[0, pl.ds(start, nk), :]
        vb = v_ref[0, pl.ds(start, nk), :]
        col = lax.broadcasted_iota(jnp.int32, (1, nk), 1)
        pen = jnp.where(col >= pad_rows - c * mq, 0.0, NEG_INF)
    else:
        kb = k_ref[0]
        vb = v_ref[0]
        pen = None
    lane = lax.broadcasted_iota(jnp.int32, q2.shape, 1)
    outs = []
    for hh in range(2):
        in_head = (lane < HEAD) if hh == 0 else (lane >= HEAD)
        qm = jnp.where(in_head, q2, jnp.zeros_like(q2))
        s = _dot_nt(qm, kb) + tbl_ref[hh]
        if pen is not None:
            s = s + pen
        m = jnp.max(s, axis=-1, keepdims=True)
        p = jnp.exp(s - m)
        l = jnp.sum(p, axis=-1, keepdims=True)
        o = _dot(p.astype(BF16), vb)
        outs.append(o * (1.0 / l))
    lane_o = lax.broadcasted_iota(jnp.int32, outs[0].shape, 1)
    o_ref[0] = jnp.where(lane_o < HEAD, outs[0], outs[1]).astype(o_ref.dtype)


def _band_attention(q, kpad, vpad, tbl, *, mq, nk, windowed, pad_rows):
    B, T, D = q.shape
    Tk = kpad.shape[1]
    n_pairs = D // (2 * HEAD)
    assert T % mq == 0
    q_spec = pl.BlockSpec((1, mq, 2 * HEAD), lambda hp, b, c: (b, c, hp))
    kv_spec = pl.BlockSpec((1, Tk, 2 * HEAD), lambda hp, b, c: (b, 0, hp))
    tbl_spec = pl.BlockSpec((2, mq, nk), lambda hp, b, c: (hp, 0, 0))
    return pl.pallas_call(
        functools.partial(_band_attn_kernel, nk=nk, windowed=windowed, pad_rows=pad_rows),
        grid=(n_pairs, B, T // mq),
        in_specs=[q_spec, kv_spec, kv_spec, tbl_spec],
        out_specs=q_spec,
        out_shape=jax.ShapeDtypeStruct((B, T, D), BF16),
        compiler_params=pltpu.CompilerParams(
            dimension_semantics=("parallel", "parallel", "arbitrary"), vmem_limit_bytes=VMEM_LIMIT),
        name="band_attn",
    )(q, kpad, vpad, tbl)


def _bias_table(rel_bias, mq, nk, banded):
    i = jnp.arange(mq, dtype=jnp.int32)[:, None]
    j = jnp.arange(nk, dtype=jnp.int32)[None, :]
    rel = jnp.clip(i - j + ATT_WINDOW, -(CHUNK - 1), REL_MAX_PAST) + (CHUNK - 1)
    tbl = rel_bias.astype(F32)[:, rel]
    if banded:
        qc, kc = i // CHUNK, j // CHUNK
        allowed = (kc >= qc) & (kc <= qc + LEFT_CHUNKS)
        tbl = jnp.where(allowed[None], tbl, NEG_INF)
    return tbl


def _mix_ffn_kernel(*refs, rwkv, final, n_ff_chunks):
    it = iter(refs)
    h_ref = next(it)
    if rwkv:
        o_ref, bonus_ref, g_ref = next(it), next(it), next(it)
        lnw_ref, lnb_ref, seg_ref, exp_ref = next(it), next(it), next(it), next(it)
    else:
        a_ref = next(it)
    wo_ref, gf_ref, wgu_ref, wd_ref = next(it), next(it), next(it), next(it)
    gfin_ref = next(it) if final else None
    out_ref = next(it)

    if rwkv:
        o = o_ref[...]
        mean = _seg_expand(_seg_sum(o, seg_ref) * (1.0 / HEAD), exp_ref)
        d = o - mean
        rstd = lax.rsqrt(_seg_sum(d * d, seg_ref) * (1.0 / HEAD) + LNX_EPS)
        y = d * _seg_expand(rstd, exp_ref) * lnw_ref[...] + lnb_ref[...]
        a = ((y + bonus_ref[...]) * g_ref[...]).astype(BF16)
    else:
        a = a_ref[...]
    h1 = h_ref[...] + _dot(a, wo_ref[...])

    xn = _rms(h1, gf_ref[...]).astype(BF16)
    dff = wd_ref.shape[0]
    fc = dff // n_ff_chunks
    acc = h1
    for c in range(n_ff_chunks):
        gate = _dot(xn, wgu_ref[:, c * fc:(c + 1) * fc])
        up = _dot(xn, wgu_ref[:, dff + c * fc:dff + (c + 1) * fc])
        act = (gate * jax.nn.sigmoid(gate) * up).astype(BF16)
        acc = acc + _dot(act, wd_ref[c * fc:(c + 1) * fc, :])
    if final:
        acc = _rms(acc, gfin_ref[...])
    out_ref[...] = acc


def _mix_ffn(h, mix_inputs, mix_weights, w_out, gf, w_gu, w_down, g_final, *, rwkv, tm, n_ff_chunks):
    N, D = h.shape
    assert N % tm == 0
    final = g_final is not None
    row = pl.BlockSpec((tm, D), lambda i: (i, 0))
    consts = list(mix_weights) + [w_out, gf, w_gu, w_down] + ([g_final] if final else [])
    return pl.pallas_call(
        functools.partial(_mix_ffn_kernel, rwkv=rwkv, final=final, n_ff_chunks=n_ff_chunks),
        grid=(N // tm,),
        in_specs=[row] + [row] * len(mix_inputs) + [_const_spec(w.shape) for w in consts],
        out_specs=row,
        out_shape=jax.ShapeDtypeStruct((N, D), F32),
        compiler_params=pltpu.CompilerParams(
            dimension_semantics=("parallel",), vmem_limit_bytes=VMEM_LIMIT),
        name="mix_ffn_rwkv" if rwkv else "mix_ffn_attn",
    )(h, *mix_inputs, *consts)


def _row_block(n, want):
    t = min(want, n)
    while n % t:
        t //= 2
    return t


def kernel(x_prompt, x_sample, state_wkv, state_shift, cache_k, cache_v, norm_mix, norm_ffn, norm_final, rwkv_mu, rwkv_w_rkv, rwkv_w_out, rwkv_decay_w0, rwkv_decay_w1, rwkv_decay_w2, rwkv_iclr_a0, rwkv_iclr_a1, rwkv_iclr_a2, rwkv_vres_v0, rwkv_vres_v1, rwkv_vres_v2, rwkv_gate_g1, rwkv_gate_g2, rwkv_k_k, rwkv_k_a, rwkv_r_k, rwkv_lnx_w, rwkv_lnx_b, attn_w_qkv, attn_w_out, attn_rel_bias, ffn_w_gu, ffn_w_down):
    B, T, D = x_prompt.shape
    Bs, Ts, _ = x_sample.shape
    depth = norm_mix.shape[0]
    heads = D // HEAD
    W = cache_k.shape[2]
    assert W == ATT_WINDOW and Ts <= CHUNK and PAST_LEN % CHUNK == 0

    vec = lambda a: a.reshape(1, D).astype(F32)
    bf = lambda a: a.astype(BF16)

    col = jnp.arange(D, dtype=jnp.int32)[:, None] // HEAD
    seg = (col == jnp.arange(SEG_LANES, dtype=jnp.int32)[None, :]).astype(BF16)
    exp = jnp.concatenate([seg.T, seg.T], axis=0)

    hp, hs = x_prompt, x_sample
    vf_p = vf_s = None
    wkv_p, shift_p, k_p, v_p = [], [], [], []
    wkv_s, shift_s, k_s, v_s = [], [], [], []

    tm_proj = _row_block(T, 256)
    tm_ffn_p = _row_block(B * T, 256)
    tm_ffn_s = _row_block(Bs * Ts, 512)
    tq = _row_block(min(T, ATT_WINDOW), 512)
    group = _row_block(T, 4 * CHUNK)
    nk_p = ATT_WINDOW + group
    tbl_shapes = (group, nk_p), (Ts, W + Ts)

    for layer in range(depth):
        j = layer // 2
        gn = vec(norm_mix[layer])
        last = layer == depth - 1
        ffn_args = dict(gf=vec(norm_ffn[layer]), w_gu=bf(ffn_w_gu[layer]), w_down=bf(ffn_w_down[layer]),
                        g_final=vec(norm_final) if last else None, n_ff_chunks=2)
        if layer % 2 == 0:
            p = dict(mu=jnp.pad(rwkv_mu[j].astype(F32), ((0, 2), (0, 0))), w_rkv=bf(rwkv_w_rkv[j]),
                     w0=vec(rwkv_decay_w0[j]), w1=bf(rwkv_decay_w1[j]), w2=bf(rwkv_decay_w2[j]),
                     a0=vec(rwkv_iclr_a0[j]), a1=bf(rwkv_iclr_a1[j]), a2=bf(rwkv_iclr_a2[j]),
                     g1=bf(rwkv_gate_g1[j]), g2=bf(rwkv_gate_g2[j]),
                     k_k=vec(rwkv_k_k[j]), k_a=vec(rwkv_k_a[j]), r_k=vec(rwkv_r_k[j]))
            if j > 0:
                p.update(v0=vec(rwkv_vres_v0[j - 1]), v1=bf(rwkv_vres_v1[j - 1]), v2=bf(rwkv_vres_v2[j - 1]))
            lnw, lnb, w_out = vec(rwkv_lnx_w[j]), vec(rwkv_lnx_b[j]), bf(rwkv_w_out[j])

            def rwkv_side(h, shift, s0, vfirst, tm_proj, chunk, tm_ffn):
                b_, t_, _ = h.shape
                (r, k, v, lw, kk, bb, g, bonus), x_last = _tmix_proj(
                    h, shift, vfirst, gn, p, seg, exp, tm=tm_proj)
                o, s_new = _wkv_chunked(r, k, v, lw, kk, bb, s0, chunk=chunk)
                rows = lambda a: a.reshape(b_ * t_, D)
                h_new = _mix_ffn(rows(h), (rows(o), rows(bonus), rows(g)), (lnw, lnb, seg, exp), w_out,
                                 rwkv=True, tm=tm_ffn, **ffn_args)
                return h_new.reshape(h.shape), x_last, s_new, (v if vfirst is None else vfirst)

            hp, sh, st, vf_p = rwkv_side(hp, jnp.zeros((B, D), F32), jnp.zeros((B, heads, HEAD, HEAD), F32),
                                         vf_p, tm_proj, _row_block(T, WKV_CHUNK), tm_ffn_p)
            wkv_p.append(st); shift_p.append(sh)
            hs, sh, st, vf_s = rwkv_side(hs, state_shift[j], state_wkv[j].astype(F32),
                                         vf_s, tm_ffn_s, _row_block(Ts, WKV_CHUNK), tm_ffn_s)
            wkv_s.append(st); shift_s.append(sh)
        else:
            w_qkv, w_out = bf(attn_w_qkv[j]), bf(attn_w_out[j])
            q, kpad, vpad, k_last, v_last = _qkv_proj(hp, gn, w_qkv, tq=tq, pad_rows=ATT_WINDOW,
                                                      keep=min(ATT_WINDOW, T))
            tbl = _bias_table(attn_rel_bias[j], *tbl_shapes[0], banded=True)
            att = _band_attention(q, kpad, vpad, tbl, mq=group, nk=nk_p, windowed=True, pad_rows=ATT_WINDOW)
            hp = _mix_ffn(hp.reshape(B * T, D), (att.reshape(B * T, D),), (), w_out,
                          rwkv=False, tm=tm_ffn_p, **ffn_args).reshape(B, T, D)
            k_p.append(k_last.reshape(B, -1, heads, HEAD)); v_p.append(v_last.reshape(B, -1, heads, HEAD))
            ns = Bs * Ts
            q, k16, v16, k_new, v_new = _qkv_proj(hs.reshape(1, ns, D), gn, w_qkv, tq=tm_ffn_s, pad_rows=0, keep=ns)
            keys = jnp.concatenate([bf(cache_k[j]).reshape(Bs, W, D), k16.reshape(Bs, Ts, D)], axis=1)
            vals = jnp.concatenate([bf(cache_v[j]).reshape(Bs, W, D), v16.reshape(Bs, Ts, D)], axis=1)
            tbl = _bias_table(attn_rel_bias[j], *tbl_shapes[1], banded=False)
            att = _band_attention(q.reshape(Bs, Ts, D), keys, vals, tbl, mq=Ts, nk=W + Ts, windowed=False,
                                  pad_rows=0)
            hs = _mix_ffn(hs.reshape(ns, D), (att.reshape(ns, D),), (), w_out,
                          rwkv=False, tm=tm_ffn_s, **ffn_args).reshape(Bs, Ts, D)
            k_s.append(k_new.reshape(Bs, Ts, heads, HEAD)); v_s.append(v_new.reshape(Bs, Ts, heads, HEAD))

    return (hp, hs,
            jnp.stack(wkv_p), jnp.stack(shift_p), jnp.stack(k_p), jnp.stack(v_p),
            jnp.stack(wkv_s), jnp.stack(shift_s), jnp.stack(k_s), jnp.stack(v_s))
```

```python
import functools

import jax
import jax.numpy as jnp
from jax import lax
from jax.experimental import pallas as pl
from jax.experimental.pallas import tpu as pltpu

F32 = jnp.float32
BF16 = jnp.bfloat16

HEAD = 64
CHUNK = 64
LEFT_CHUNKS = 8
ATT_WINDOW = LEFT_CHUNKS * CHUNK
REL_MAX_PAST = 256
PAST_LEN = 4096
RMS_EPS = 1e-6
LNX_EPS = 64e-5
NEG_INF = -1e30
WKV_CHUNK = 64
SEG_LANES = 128
VMEM_LIMIT = 56 * 1024 * 1024


def _dot(a, b):
    return jnp.dot(a, b, preferred_element_type=F32)


def _dot_nt(a, b):
    return lax.dot_general(a, b, (((1,), (1,)), ((), ())), preferred_element_type=F32)


def _dot_tn(a, b):
    return lax.dot_general(a, b, (((0,), (0,)), ((), ())), preferred_element_type=F32)


def _rms(x, g):
    return x * lax.rsqrt(jnp.mean(x * x, axis=-1, keepdims=True) + RMS_EPS) * g


def _const_spec(shape):
    nd = len(shape)
    return pl.BlockSpec(shape, lambda *_: (0,) * nd, pipeline_mode=pl.Buffered(1))


def _seg_sum(x, seg_ref):
    return _dot(x.astype(BF16), seg_ref[...])


def _seg_expand(s, exp_ref):
    hi = s.astype(BF16)
    lo = (s - hi.astype(F32)).astype(BF16)
    return _dot(jnp.concatenate([hi, lo], axis=-1), exp_ref[...])


def _tmix_proj_kernel(*refs, rows_mode, has_vres, seq_blocks):
    it = iter(refs)
    h_ref = next(it)
    bnd_ref = next(it)
    shift_ref = None if rows_mode else next(it)
    vfirst_ref = next(it) if has_vres else None
    gn_ref = next(it)
    mu_ref = next(it)
    wrkv_ref = next(it)
    w0_ref, w1_ref, w2_ref = next(it), next(it), next(it)
    a0_ref, a1_ref, a2_ref = next(it), next(it), next(it)
    if has_vres:
        v0_ref, v1_ref, v2_ref = next(it), next(it), next(it)
    g1_ref, g2_ref = next(it), next(it)
    kk_ref, ka_ref, rk_ref = next(it), next(it), next(it)
    seg_ref, exp_ref = next(it), next(it)
    (r_out, k_out, v_out, lw_out, kk_out, bb_out, g_out, bonus_out, xl_out) = it

    gn = gn_ref[...]
    x = _rms(h_ref[0], gn)
    tm = x.shape[0]
    row = lax.broadcasted_iota(jnp.int32, x.shape, 0)
    rolled = pltpu.roll(x, 1, 0)
    if rows_mode:
        seq_len = seq_blocks
        x_prev = jnp.where(row % seq_len == 0, bnd_ref[0], rolled)
        xl_out[0] = x
    else:
        halo = bnd_ref[0]
        prev_row = _rms(halo[7:8, :], gn)
        first = pl.program_id(1) % seq_blocks == 0
        prev_row = jnp.where(first, shift_ref[0], prev_row)
        x_prev = jnp.where(row == 0, prev_row, rolled)
        xl_out[0] = x[tm - 1:tm, :]
    xx = x_prev - x

    def mixed(j):
        return (x + xx * mu_ref[j:j + 1, :]).astype(BF16)

    m_v = mixed(2)
    r = _dot(mixed(0), wrkv_ref[0])
    k = _dot(mixed(1), wrkv_ref[1])
    v = _dot(m_v, wrkv_ref[2])

    w_pre = w0_ref[...] + _dot(jnp.tanh(_dot(mixed(3), w1_ref[...])).astype(BF16), w2_ref[...])
    lw = jax.nn.sigmoid(w_pre) * (-0.6065306597126334)
    a = jax.nn.sigmoid(a0_ref[...] + _dot(_dot(mixed(4), a1_ref[...]).astype(BF16), a2_ref[...]))
    g = _dot(jax.nn.sigmoid(_dot(mixed(5), g1_ref[...])).astype(BF16), g2_ref[...])
    if has_vres:
        gate = jax.nn.sigmoid(v0_ref[...] + _dot(_dot(m_v, v1_ref[...]).astype(BF16), v2_ref[...]))
        v = v + (vfirst_ref[0] - v) * gate

    kk = k * kk_ref[...]
    norm = jnp.maximum(jnp.sqrt(_seg_sum(kk * kk, seg_ref)), 1e-12)
    kk = kk * _seg_expand(1.0 / norm, exp_ref)
    k = k * (1.0 + (a - 1.0) * ka_ref[...])
    bonus = _seg_expand(_seg_sum(r * k * rk_ref[...], seg_ref), exp_ref) * v

    r_out[0] = r
    k_out[0] = k
    v_out[0] = v
    lw_out[0] = lw
    kk_out[0] = kk
    bb_out[0] = kk * a
    g_out[0] = g
    bonus_out[0] = bonus


def _tmix_proj(h, shift, vfirst, gn, p, seg, exp, *, tm):
    B, T, D = h.shape
    rows_mode = T < tm
    has_vres = vfirst is not None
    if rows_mode:
        nseq = tm // T
        assert (B * T) % tm == 0
        hb = h.reshape(B * T // tm, tm, D)
        bnd = jnp.zeros((B, T, D), F32).at[:, 0, :].set(shift).reshape(hb.shape)
        grid = (hb.shape[0], 1)
        blk = pl.BlockSpec((1, tm, D), lambda b, i: (b, 0, 0))
        in_arrays = [hb, bnd]
        in_specs = [blk, blk]
        seq_blocks = T
        if has_vres:
            in_arrays.append(vfirst.reshape(hb.shape))
            in_specs.append(blk)
        xl_shape = jax.ShapeDtypeStruct(hb.shape, F32)
        xl_spec = blk
        out_shape_main = hb.shape
    else:
        assert T % tm == 0 and tm % 8 == 0
        nblk = T // tm
        grid = (B, nblk)
        blk = pl.BlockSpec((1, tm, D), lambda b, i: (b, i, 0))
        halo = pl.BlockSpec((1, 8, D), lambda b, i: (b, jnp.maximum(i * (tm // 8) - 1, 0), 0))
        in_arrays = [h, h, shift.reshape(B, 1, D)]
        in_specs = [blk, halo, pl.BlockSpec((1, 1, D), lambda b, i: (b, 0, 0))]
        seq_blocks = nblk
        if has_vres:
            in_arrays.append(vfirst)
            in_specs.append(blk)
        xl_shape = jax.ShapeDtypeStruct((B, 1, D), F32)
        xl_spec = pl.BlockSpec((1, 1, D), lambda b, i: (b, 0, 0))
        out_shape_main = h.shape

    weights = [gn, p["mu"], p["w_rkv"], p["w0"], p["w1"], p["w2"], p["a0"], p["a1"], p["a2"]]
    if has_vres:
        weights += [p["v0"], p["v1"], p["v2"]]
    weights += [p["g1"], p["g2"], p["k_k"], p["k_a"], p["r_k"], seg, exp]
    in_arrays += weights
    in_specs += [_const_spec(w.shape) for w in weights]

    main = jax.ShapeDtypeStruct(out_shape_main, F32)
    outs = pl.pallas_call(
        functools.partial(_tmix_proj_kernel, rows_mode=rows_mode, has_vres=has_vres, seq_blocks=seq_blocks),
        grid=grid,
        in_specs=in_specs,
        out_specs=[blk] * 8 + [xl_spec],
        out_shape=[main] * 8 + [xl_shape],
        compiler_params=pltpu.CompilerParams(
            dimension_semantics=("parallel", "arbitrary"), vmem_limit_bytes=VMEM_LIMIT),
        name="tmix_proj",
    )(*in_arrays)
    main_outs = [o.reshape(B, T, D) for o in outs[:8]]
    if rows_mode:
        x_last = outs[8].reshape(B, T, D)[:, T - 1, :]
    else:
        x_last = outs[8].reshape(B, D)
    return main_outs, x_last


def _wkv_chunk_kernel(r_ref, k_ref, v_ref, lw_ref, kk_ref, bb_ref, s0_ref, o_ref, s_out_ref, s_scr, *, heads):
    c = pl.program_id(1)
    C = r_ref.shape[1]

    @pl.when(c == 0)
    def _():
        s_scr[...] = s0_ref[0]

    lw = lw_ref[0]
    ti = lax.broadcasted_iota(jnp.int32, (C, C), 0)
    si = lax.broadcasted_iota(jnp.int32, (C, C), 1)
    tril_incl = ti >= si
    tril_strict = ti > si
    cs = jnp.dot(tril_incl.astype(F32), lw, preferred_element_type=F32, precision=lax.Precision.HIGHEST)
    p_incl = jnp.exp(cs)
    p_inv = jnp.exp(-cs)
    p_prev = jnp.exp(cs - lw)
    p_last = p_incl[C - 1:C, :]

    rt = (r_ref[0] * p_incl).astype(BF16)
    nt = (kk_ref[0] * (-p_prev)).astype(BF16)
    bt = (bb_ref[0] * p_inv).astype(BF16)
    kt = (k_ref[0] * p_inv).astype(BF16)
    vv = v_ref[0].astype(BF16)

    hs = range(heads)
    sls = [slice(h * HEAD, (h + 1) * HEAD) for h in hs]
    nr = [jnp.concatenate([nt[:, sl], rt[:, sl]], axis=0) for sl in sls]
    b_h = [bt[:, sl] for sl in sls]
    k_h = [kt[:, sl] for sl in sls]
    v_h = [vv[:, sl] for sl in sls]
    s_prev = [s_scr[h] for h in hs]
    gb = [_dot_nt(nr[h], b_h[h]) for h in hs]
    gk = [_dot_nt(nr[h], k_h[h]) for h in hs]
    x0 = [_dot_nt(nr[h], s_prev[h].astype(BF16)) for h in hs]
    a_k = [jnp.where(tril_strict, gk[h][:C], 0.0).astype(BF16) for h in hs]
    u = [x0[h][:C] + _dot(a_k[h], v_h[h]) for h in hs]
    apow = [jnp.where(tril_strict, gb[h][:C], 0.0).astype(BF16) for h in hs]
    n_stage = C.bit_length() - 1
    for i in range(n_stage):
        u = [u[h] + _dot(apow[h], u[h].astype(BF16)) for h in hs]
        if i + 1 < n_stage:
            apow = [_dot(apow[h], apow[h]).astype(BF16) for h in hs]
    u16 = [u[h].astype(BF16) for h in hs]
    l_b = [jnp.where(tril_incl, gb[h][C:], 0.0).astype(BF16) for h in hs]
    l_k = [jnp.where(tril_incl, gk[h][C:], 0.0).astype(BF16) for h in hs]
    for h in hs:
        o_ref[0, :, sls[h]] = x0[h][C:] + _dot(l_b[h], u16[h]) + _dot(l_k[h], v_h[h])
    for h in hs:
        upd = _dot_tn(jnp.concatenate([u16[h], v_h[h]], axis=0), jnp.concatenate([b_h[h], k_h[h]], axis=0))
        s_scr[h] = (s_prev[h] + upd) * p_last[:, sls[h]]

    @pl.when(c == pl.num_programs(1) - 1)
    def _():
        s_out_ref[0] = s_scr[...]


def _wkv_chunked(r, k, v, lw, kk, bb, s0, *, chunk):
    B, T, D = r.shape
    heads = D // HEAD
    assert T % chunk == 0
    blk = pl.BlockSpec((1, chunk, D), lambda b, c: (b, c, 0))
    st = pl.BlockSpec((1, heads, HEAD, HEAD), lambda b, c: (b, 0, 0, 0))
    o, s_out = pl.pallas_call(
        functools.partial(_wkv_chunk_kernel, heads=heads),
        grid=(B, T // chunk),
        in_specs=[blk] * 6 + [st],
        out_specs=[blk, st],
        out_shape=[jax.ShapeDtypeStruct((B, T, D), F32), jax.ShapeDtypeStruct(s0.shape, F32)],
        scratch_shapes=[pltpu.VMEM((heads, HEAD, HEAD), F32)],
        compiler_params=pltpu.CompilerParams(
            dimension_semantics=("parallel", "arbitrary"), vmem_limit_bytes=VMEM_LIMIT),
        name="wkv_chunk",
    )(r, k, v, lw, kk, bb, s0)
    return o, s_out


def _qkv_kernel(h_ref, gn_ref, w_ref, q_out, kp_out, vp_out, kl_out, vl_out, *, npad, last_from):
    i = pl.program_id(1)
    D = h_ref.shape[2]

    @pl.when(i < npad)
    def _():
        kp_out[0] = jnp.zeros(kp_out.shape[1:], kp_out.dtype)
        vp_out[0] = jnp.zeros(vp_out.shape[1:], vp_out.dtype)

    @pl.when(i >= npad)
    def _():
        xn = _rms(h_ref[0], gn_ref[...]).astype(BF16)
        q = _dot(xn, w_ref[:, 0:D])
        k = _dot(xn, w_ref[:, D:2 * D])
        v = _dot(xn, w_ref[:, 2 * D:3 * D])
        q_out[0] = (q * (HEAD ** -0.5)).astype(BF16)
        kp_out[0] = k.astype(BF16)
        vp_out[0] = v.astype(BF16)

        @pl.when(i >= npad + last_from)
        def _():
            kl_out[0] = k
            vl_out[0] = v


def _qkv_proj(h, gn, w_qkv, *, tq, pad_rows, keep):
    B, T, D = h.shape
    assert T % tq == 0 and pad_rows % tq == 0
    npad = pad_rows // tq
    assert keep % tq == 0
    last_from = (T - keep) // tq
    blk_in = pl.BlockSpec((1, tq, D), lambda b, i: (b, jnp.maximum(i - npad, 0), 0))
    blk_pad = pl.BlockSpec((1, tq, D), lambda b, i: (b, i, 0))
    blk_last = pl.BlockSpec((1, tq, D), lambda b, i: (b, jnp.maximum(i - npad - last_from, 0), 0))
    return pl.pallas_call(
        functools.partial(_qkv_kernel, npad=npad, last_from=last_from),
        grid=(B, npad + T // tq),
        in_specs=[blk_in, _const_spec(gn.shape), _const_spec(w_qkv.shape)],
        out_specs=[blk_in, blk_pad, blk_pad, blk_last, blk_last],
        out_shape=[jax.ShapeDtypeStruct((B, T, D), BF16),
                   jax.ShapeDtypeStruct((B, pad_rows + T, D), BF16),
                   jax.ShapeDtypeStruct((B, pad_rows + T, D), BF16),
                   jax.ShapeDtypeStruct((B, keep, D), F32),
                   jax.ShapeDtypeStruct((B, keep, D), F32)],
        compiler_params=pltpu.CompilerParams(
            dimension_semantics=("parallel", "arbitrary"), vmem_limit_bytes=VMEM_LIMIT),
        name="qkv_proj",
    )(h, gn, w_qkv)


def _band_attn_kernel(q_ref, k_ref, v_ref, tbl_ref, o_ref, *, nk, windowed, pad_rows):
    mq = q_ref.shape[1]
    q2 = q_ref[0]
    if windowed:
        c = pl.program_id(2)
        start = pl.multiple_of(c * mq, mq)
        kb = k_ref[0, pl.ds(start, nk), :]
        vb = v_ref[0, pl.ds(start, nk), :]
        col = lax.broadcasted_iota(jnp.int32, (1, nk), 1)
        pen = jnp.where(col >= pad_rows - c * mq, 0.0, NEG_INF)
    else:
        kb = k_ref[0]
        vb = v_ref[0]
        pen = None
    lane = lax.broadcasted_iota(jnp.int32, q2.shape, 1)
    outs = []
    for hh in range(2):
        in_head = (lane < HEAD) if hh == 0 else (lane >= HEAD)
        qm = jnp.where(in_head, q2, jnp.zeros_like(q2))
        s = _dot_nt(qm, kb) + tbl_ref[hh]
        if pen is not None:
            s = s + pen
        m = jnp.max(s, axis=-1, keepdims=True)
        p = jnp.exp(s - m)
        l = jnp.sum(p, axis=-1, keepdims=True)
        o = _dot(p.astype(BF16), vb)
        outs.append(o * (1.0 / l))
    lane_o = lax.broadcasted_iota(jnp.int32, outs[0].shape, 1)
    o_ref[0] = jnp.where(lane_o < HEAD, outs[0], outs[1]).astype(o_ref.dtype)


def _band_attention(q, kpad, vpad, tbl, *, mq, nk, windowed, pad_rows):
    B, T, D = q.shape
    Tk = kpad.shape[1]
    n_pairs = D // (2 * HEAD)
    assert T % mq == 0
    q_spec = pl.BlockSpec((1, mq, 2 * HEAD), lambda hp, b, c: (b, c, hp))
    kv_spec = pl.BlockSpec((1, Tk, 2 * HEAD), lambda hp, b, c: (b, 0, hp))
    tbl_spec = pl.BlockSpec((2, mq, nk), lambda hp, b, c: (hp, 0, 0))
    return pl.pallas_call(
        functools.partial(_band_attn_kernel, nk=nk, windowed=windowed, pad_rows=pad_rows),
        grid=(n_pairs, B, T // mq),
        in_specs=[q_spec, kv_spec, kv_spec, tbl_spec],
        out_specs=q_spec,
        out_shape=jax.ShapeDtypeStruct((B, T, D), BF16),
        compiler_params=pltpu.CompilerParams(
            dimension_semantics=("parallel", "parallel", "arbitrary"), vmem_limit_bytes=VMEM_LIMIT),
        name="band_attn",
    )(q, kpad, vpad, tbl)


def _bias_table(rel_bias, mq, nk, banded):
    period = nk + mq
    e = jnp.arange(period, dtype=jnp.int32)
    e = jnp.where(e <= nk, e, e - period)
    rel = jnp.clip(ATT_WINDOW - e, -(CHUNK - 1), REL_MAX_PAST) + (CHUNK - 1)
    line = rel_bias.astype(F32)[:, rel]
    tiled = jnp.tile(line, (1, mq))[:, :mq * (period - 1)]
    tbl = tiled.reshape(-1, mq, period - 1)[:, :, :nk]
    i = jnp.arange(mq, dtype=jnp.int32)[:, None]
    j = jnp.arange(nk, dtype=jnp.int32)[None, :]
    if banded:
        qc, kc = i // CHUNK, j // CHUNK
        allowed = (kc >= qc) & (kc <= qc + LEFT_CHUNKS)
        tbl = jnp.where(allowed[None], tbl, NEG_INF)
    return tbl


def _mix_ffn_kernel(*refs, rwkv, final, n_ff_chunks):
    it = iter(refs)
    h_ref = next(it)
    if rwkv:
        o_ref, bonus_ref, g_ref = next(it), next(it), next(it)
        lnw_ref, lnb_ref, seg_ref, exp_ref = next(it), next(it), next(it), next(it)
    else:
        a_ref = next(it)
    wo_ref, gf_ref, wgu_ref, wd_ref = next(it), next(it), next(it), next(it)
    gfin_ref = next(it) if final else None
    out_ref = next(it)

    if rwkv:
        o = o_ref[...]
        mean = _seg_expand(_seg_sum(o, seg_ref) * (1.0 / HEAD), exp_ref)
        d = o - mean
        rstd = lax.rsqrt(_seg_sum(d * d, seg_ref) * (1.0 / HEAD) + LNX_EPS)
        y = d * _seg_expand(rstd, exp_ref) * lnw_ref[...] + lnb_ref[...]
        a = ((y + bonus_ref[...]) * g_ref[...]).astype(BF16)
    else:
        a = a_ref[...]
    h1 = h_ref[...] + _dot(a, wo_ref[...])

    xn = _rms(h1, gf_ref[...]).astype(BF16)
    dff = wd_ref.shape[0]
    fc = dff // n_ff_chunks
    acc = h1
    for c in range(n_ff_chunks):
        gate = _dot(xn, wgu_ref[:, c * fc:(c + 1) * fc])
        up = _dot(xn, wgu_ref[:, dff + c * fc:dff + (c + 1) * fc])
        act = (gate * jax.nn.sigmoid(gate) * up).astype(BF16)
        acc = acc + _dot(act, wd_ref[c * fc:(c + 1) * fc, :])
    if final:
        acc = _rms(acc, gfin_ref[...])
    out_ref[...] = acc


def _mix_ffn(h, mix_inputs, mix_weights, w_out, gf, w_gu, w_down, g_final, *, rwkv, tm, n_ff_chunks):
    N, D = h.shape
    assert N % tm == 0
    final = g_final is not None
    row = pl.BlockSpec((tm, D), lambda i: (i, 0))
    consts = list(mix_weights) + [w_out, gf, w_gu, w_down] + ([g_final] if final else [])
    return pl.pallas_call(
        functools.partial(_mix_ffn_kernel, rwkv=rwkv, final=final, n_ff_chunks=n_ff_chunks),
        grid=(N // tm,),
        in_specs=[row] + [row] * len(mix_inputs) + [_const_spec(w.shape) for w in consts],
        out_specs=row,
        out_shape=jax.ShapeDtypeStruct((N, D), F32),
        compiler_params=pltpu.CompilerParams(
            dimension_semantics=("parallel",), vmem_limit_bytes=VMEM_LIMIT),
        name="mix_ffn_rwkv" if rwkv else "mix_ffn_attn",
    )(h, *mix_inputs, *consts)


def _row_block(n, want):
    t = min(want, n)
    while n % t:
        t //= 2
    return t


def kernel(x_prompt, x_sample, state_wkv, state_shift, cache_k, cache_v, norm_mix, norm_ffn, norm_final, rwkv_mu, rwkv_w_rkv, rwkv_w_out, rwkv_decay_w0, rwkv_decay_w1, rwkv_decay_w2, rwkv_iclr_a0, rwkv_iclr_a1, rwkv_iclr_a2, rwkv_vres_v0, rwkv_vres_v1, rwkv_vres_v2, rwkv_gate_g1, rwkv_gate_g2, rwkv_k_k, rwkv_k_a, rwkv_r_k, rwkv_lnx_w, rwkv_lnx_b, attn_w_qkv, attn_w_out, attn_rel_bias, ffn_w_gu, ffn_w_down):
    B, T, D = x_prompt.shape
    Bs, Ts, _ = x_sample.shape
    depth = norm_mix.shape[0]
    heads = D // HEAD
    W = cache_k.shape[2]
    assert W == ATT_WINDOW and Ts <= CHUNK and PAST_LEN % CHUNK == 0

    vec = lambda a: a.reshape(1, D).astype(F32)
    bf = lambda a: a.astype(BF16)

    col = jnp.arange(D, dtype=jnp.int32)[:, None] // HEAD
    seg = (col == jnp.arange(SEG_LANES, dtype=jnp.int32)[None, :]).astype(BF16)
    exp = jnp.concatenate([seg.T, seg.T], axis=0)

    hp, hs = x_prompt, x_sample
    vf_p = vf_s = None
    wkv_p, shift_p, k_p, v_p = [], [], [], []
    wkv_s, shift_s, k_s, v_s = [], [], [], []

    tm_proj = _row_block(T, 256)
    tm_ffn_p = _row_block(B * T, 256)
    tm_ffn_s = _row_block(Bs * Ts, 512)
    tq = _row_block(min(T, ATT_WINDOW), 512)
    group = _row_block(T, 4 * CHUNK)
    nk_p = ATT_WINDOW + group
    tbl_shapes = (group, nk_p), (Ts, W + Ts)

    for layer in range(depth):
        j = layer // 2
        gn = vec(norm_mix[layer])
        last = layer == depth - 1
        ffn_args = dict(gf=vec(norm_ffn[layer]), w_gu=bf(ffn_w_gu[layer]), w_down=bf(ffn_w_down[layer]),
                        g_final=vec(norm_final) if last else None, n_ff_chunks=2)
        if layer % 2 == 0:
            p = dict(mu=jnp.pad(rwkv_mu[j].astype(F32), ((0, 2), (0, 0))), w_rkv=bf(rwkv_w_rkv[j]),
                     w0=vec(rwkv_decay_w0[j]), w1=bf(rwkv_decay_w1[j]), w2=bf(rwkv_decay_w2[j]),
                     a0=vec(rwkv_iclr_a0[j]), a1=bf(rwkv_iclr_a1[j]), a2=bf(rwkv_iclr_a2[j]),
                     g1=bf(rwkv_gate_g1[j]), g2=bf(rwkv_gate_g2[j]),
                     k_k=vec(rwkv_k_k[j]), k_a=vec(rwkv_k_a[j]), r_k=vec(rwkv_r_k[j]))
            if j > 0:
                p.update(v0=vec(rwkv_vres_v0[j - 1]), v1=bf(rwkv_vres_v1[j - 1]), v2=bf(rwkv_vres_v2[j - 1]))
            lnw, lnb, w_out = vec(rwkv_lnx_w[j]), vec(rwkv_lnx_b[j]), bf(rwkv_w_out[j])

            def rwkv_side(h, shift, s0, vfirst, tm_proj, chunk, tm_ffn):
                b_, t_, _ = h.shape
                (r, k, v, lw, kk, bb, g, bonus), x_last = _tmix_proj(
                    h, shift, vfirst, gn, p, seg, exp, tm=tm_proj)
                o, s_new = _wkv_chunked(r, k, v, lw, kk, bb, s0, chunk=chunk)
                rows = lambda a: a.reshape(b_ * t_, D)
                h_new = _mix_ffn(rows(h), (rows(o), rows(bonus), rows(g)), (lnw, lnb, seg, exp), w_out,
                                 rwkv=True, tm=tm_ffn, **ffn_args)
                return h_new.reshape(h.shape), x_last, s_new, (v if vfirst is None else vfirst)

            hp, sh, st, vf_p = rwkv_side(hp, jnp.zeros((B, D), F32), jnp.zeros((B, heads, HEAD, HEAD), F32),
                                         vf_p, tm_proj, _row_block(T, WKV_CHUNK), tm_ffn_p)
            wkv_p.append(st); shift_p.append(sh)
            hs, sh, st, vf_s = rwkv_side(hs, state_shift[j], state_wkv[j].astype(F32),
                                         vf_s, tm_ffn_s, _row_block(Ts, WKV_CHUNK), tm_ffn_s)
            wkv_s.append(st); shift_s.append(sh)
        else:
            w_qkv, w_out = bf(attn_w_qkv[j]), bf(attn_w_out[j])
            q, kpad, vpad, k_last, v_last = _qkv_proj(hp, gn, w_qkv, tq=tq, pad_rows=ATT_WINDOW,
                                                      keep=min(ATT_WINDOW, T))
            tbl = _bias_table(attn_rel_bias[j], *tbl_shapes[0], banded=True)
            att = _band_attention(q, kpad, vpad, tbl, mq=group, nk=nk_p, windowed=True, pad_rows=ATT_WINDOW)
            hp = _mix_ffn(hp.reshape(B * T, D), (att.reshape(B * T, D),), (), w_out,
                          rwkv=False, tm=tm_ffn_p, **ffn_args).reshape(B, T, D)
            k_p.append(k_last.reshape(B, -1, heads, HEAD)); v_p.append(v_last.reshape(B, -1, heads, HEAD))
            ns = Bs * Ts
            q, k16, v16, k_new, v_new = _qkv_proj(hs.reshape(1, ns, D), gn, w_qkv, tq=tm_ffn_s, pad_rows=0, keep=ns)
            keys = jnp.concatenate([bf(cache_k[j]).reshape(Bs, W, D), k16.reshape(Bs, Ts, D)], axis=1)
            vals = jnp.concatenate([bf(cache_v[j]).reshape(Bs, W, D), v16.reshape(Bs, Ts, D)], axis=1)
            tbl = _bias_table(attn_rel_bias[j], *tbl_shapes[1], banded=False)
            att = _band_attention(q.reshape(Bs, Ts, D), keys, vals, tbl, mq=Ts, nk=W + Ts, windowed=False,
                                  pad_rows=0)
            hs = _mix_ffn(hs.reshape(ns, D), (att.reshape(ns, D),), (), w_out,
                          rwkv=False, tm=tm_ffn_s, **ffn_args).reshape(Bs, Ts, D)
            k_s.append(k_new.reshape(Bs, Ts, heads, HEAD)); v_s.append(v_new.reshape(Bs, Ts, heads, HEAD))

    return (hp, hs,
            jnp.stack(wkv_p), jnp.stack(shift_p), jnp.stack(k_p), jnp.stack(v_p),
            jnp.stack(wkv_s), jnp.stack(shift_s), jnp.stack(k_s), jnp.stack(v_s))
```

```python
import functools

import jax
import jax.numpy as jnp
from jax import lax
from jax.experimental import pallas as pl
from jax.experimental.pallas import tpu as pltpu

F32 = jnp.float32
BF16 = jnp.bfloat16

HEAD = 64
CHUNK = 64
LEFT_CHUNKS = 8
ATT_WINDOW = LEFT_CHUNKS * CHUNK
REL_MAX_PAST = 256
PAST_LEN = 4096
RMS_EPS = 1e-6
LNX_EPS = 64e-5
NEG_INF = -1e30
LOG2E = 1.4426950408889634
WKV_CHUNK = 64
WKV_SUB = 2
SEG_LANES = 128
VMEM_LIMIT = 56 * 1024 * 1024


def _dot(a, b):
    return jnp.dot(a, b, preferred_element_type=F32)


def _dot_nt(a, b):
    return lax.dot_general(a, b, (((1,), (1,)), ((), ())), preferred_element_type=F32)


def _dot_tn(a, b):
    return lax.dot_general(a, b, (((0,), (0,)), ((), ())), preferred_element_type=F32)


def _rms(x, g):
    return x * lax.rsqrt(jnp.mean(x * x, axis=-1, keepdims=True) + RMS_EPS) * g


def _const_spec(shape):
    nd = len(shape)
    return pl.BlockSpec(shape, lambda *_: (0,) * nd, pipeline_mode=pl.Buffered(1))


def _seg_sum(x, seg_ref):
    return _dot(x.astype(BF16), seg_ref[...])


def _seg_expand(s, exp_ref):
    hi = s.astype(BF16)
    lo = (s - hi.astype(F32)).astype(BF16)
    return _dot(jnp.concatenate([hi, lo], axis=-1), exp_ref[...])


def _tmix_proj_kernel(*refs, rows_mode, has_vres, seq_blocks):
    it = iter(refs)
    h_ref = next(it)
    bnd_ref = next(it)
    shift_ref = None if rows_mode else next(it)
    vfirst_ref = next(it) if has_vres else None
    gn_ref = next(it)
    mu_ref = next(it)
    wrkv_ref = next(it)
    w0_ref, w1_ref, w2_ref = next(it), next(it), next(it)
    a0_ref, a1_ref, a2_ref = next(it), next(it), next(it)
    if has_vres:
        v0_ref, v1_ref, v2_ref = next(it), next(it), next(it)
    g1_ref, g2_ref = next(it), next(it)
    kk_ref, ka_ref, rk_ref = next(it), next(it), next(it)
    seg_ref, exp_ref = next(it), next(it)
    (r_out, k_out, v_out, lw_out, kk_out, bb_out, g_out, bonus_out, xl_out) = it

    gn = gn_ref[...]
    x = _rms(h_ref[0], gn)
    tm = x.shape[0]
    row = lax.broadcasted_iota(jnp.int32, x.shape, 0)
    rolled = pltpu.roll(x, 1, 0)
    if rows_mode:
        seq_len = seq_blocks
        x_prev = jnp.where(row % seq_len == 0, bnd_ref[0], rolled)
        xl_out[0] = x
    else:
        halo = bnd_ref[0]
        prev_row = _rms(halo[7:8, :], gn)
        first = pl.program_id(1) % seq_blocks == 0
        prev_row = jnp.where(first, shift_ref[0], prev_row)
        x_prev = jnp.where(row == 0, prev_row, rolled)
        xl_out[0] = x[tm - 1:tm, :]
    xx = x_prev - x

    def mixed(j):
        return (x + xx * mu_ref[j:j + 1, :]).astype(BF16)

    m_v = mixed(2)
    r = _dot(mixed(0), wrkv_ref[0])
    k = _dot(mixed(1), wrkv_ref[1])
    v = _dot(m_v, wrkv_ref[2])

    w_pre = w0_ref[...] + _dot(jnp.tanh(_dot(mixed(3), w1_ref[...])).astype(BF16), w2_ref[...])
    lw = jax.nn.sigmoid(w_pre) * (-0.6065306597126334)
    a = jax.nn.sigmoid(a0_ref[...] + _dot(_dot(mixed(4), a1_ref[...]).astype(BF16), a2_ref[...]))
    g = _dot(jax.nn.sigmoid(_dot(mixed(5), g1_ref[...])).astype(BF16), g2_ref[...])
    if has_vres:
        gate = jax.nn.sigmoid(v0_ref[...] + _dot(_dot(m_v, v1_ref[...]).astype(BF16), v2_ref[...]))
        v = v + (vfirst_ref[0] - v) * gate

    kk = k * kk_ref[...]
    norm = jnp.maximum(jnp.sqrt(_seg_sum(kk * kk, seg_ref)), 1e-12)
    kk = kk * _seg_expand(1.0 / norm, exp_ref)
    k = k * (1.0 + (a - 1.0) * ka_ref[...])
    bonus = _seg_expand(_seg_sum(r * k * rk_ref[...], seg_ref), exp_ref) * v

    r_out[0] = r
    k_out[0] = k
    v_out[0] = v
    lw_out[0] = lw
    kk_out[0] = kk
    bb_out[0] = kk * a
    g_out[0] = g
    bonus_out[0] = bonus


def _tmix_proj(h, shift, vfirst, gn, p, seg, exp, *, tm):
    B, T, D = h.shape
    rows_mode = T < tm
    has_vres = vfirst is not None
    if rows_mode:
        nseq = tm // T
        assert (B * T) % tm == 0
        hb = h.reshape(B * T // tm, tm, D)
        bnd = jnp.zeros((B, T, D), F32).at[:, 0, :].set(shift).reshape(hb.shape)
        grid = (hb.shape[0], 1)
        blk = pl.BlockSpec((1, tm, D), lambda b, i: (b, 0, 0))
        in_arrays = [hb, bnd]
        in_specs = [blk, blk]
        seq_blocks = T
        if has_vres:
            in_arrays.append(vfirst.reshape(hb.shape))
            in_specs.append(blk)
        xl_shape = jax.ShapeDtypeStruct(hb.shape, F32)
        xl_spec = blk
        out_shape_main = hb.shape
    else:
        assert T % tm == 0 and tm % 8 == 0
        nblk = T // tm
        grid = (B, nblk)
        blk = pl.BlockSpec((1, tm, D), lambda b, i: (b, i, 0))
        halo = pl.BlockSpec((1, 8, D), lambda b, i: (b, jnp.maximum(i * (tm // 8) - 1, 0), 0))
        in_arrays = [h, h, shift.reshape(B, 1, D)]
        in_specs = [blk, halo, pl.BlockSpec((1, 1, D), lambda b, i: (b, 0, 0))]
        seq_blocks = nblk
        if has_vres:
            in_arrays.append(vfirst)
            in_specs.append(blk)
        xl_shape = jax.ShapeDtypeStruct((B, 1, D), F32)
        xl_spec = pl.BlockSpec((1, 1, D), lambda b, i: (b, 0, 0))
        out_shape_main = h.shape

    weights = [gn, p["mu"], p["w_rkv"], p["w0"], p["w1"], p["w2"], p["a0"], p["a1"], p["a2"]]
    if has_vres:
        weights += [p["v0"], p["v1"], p["v2"]]
    weights += [p["g1"], p["g2"], p["k_k"], p["k_a"], p["r_k"], seg, exp]
    in_arrays += weights
    in_specs += [_const_spec(w.shape) for w in weights]

    main = jax.ShapeDtypeStruct(out_shape_main, F32)
    outs = pl.pallas_call(
        functools.partial(_tmix_proj_kernel, rows_mode=rows_mode, has_vres=has_vres, seq_blocks=seq_blocks),
        grid=grid,
        in_specs=in_specs,
        out_specs=[blk] * 8 + [xl_spec],
        out_shape=[main] * 8 + [xl_shape],
        compiler_params=pltpu.CompilerParams(
            dimension_semantics=("parallel", "arbitrary"), vmem_limit_bytes=VMEM_LIMIT),
        name="tmix_proj",
    )(*in_arrays)
    main_outs = [o.reshape(B, T, D) for o in outs[:8]]
    if rows_mode:
        x_last = outs[8].reshape(B, T, D)[:, T - 1, :]
    else:
        x_last = outs[8].reshape(B, D)
    return main_outs, x_last


def _wkv_chunk_kernel(r_ref, k_ref, v_ref, lw_ref, kk_ref, bb_ref, s0_ref, o_ref, s_out_ref, s_scr, *, heads):
    c = pl.program_id(1)
    C = HEAD
    n_sub = r_ref.shape[1] // C
    W2 = 2 * HEAD

    @pl.when(c == 0)
    def _():
        s_scr[...] = s0_ref[0]

    lane = lax.broadcasted_iota(jnp.int32, (C, W2), 1)
    trow2 = lax.broadcasted_iota(jnp.int32, (C, W2), 0)
    s_idx = lane & (HEAD - 1)
    strict2 = trow2 > s_idx
    incl2 = trow2 >= s_idx
    even_lanes = lane < HEAD
    own = [even_lanes, jnp.logical_not(even_lanes)]
    eye_hi = jnp.where(trow2 == lane - HEAD, 1.0, 0.0)
    zero16 = jnp.zeros((C, W2), BF16)
    pairs = range(heads // 2)
    cols = [slice(p * W2, (p + 1) * W2) for p in pairs]
    hp = [(p, q) for p in pairs for q in range(2)]
    trow = lax.broadcasted_iota(jnp.int32, (C, r_ref.shape[2]), 0)

    n_stage = C.bit_length() - 1
    zero_s = jnp.zeros((HEAD, W2), BF16)

    def phase_a(ci, sub):
        rows = slice(ci * C, (ci + 1) * C)
        lw = lw_ref[0, rows, :]
        cs = lw
        d = 1
        while d < C:
            cs = cs + jnp.where(trow >= d, pltpu.roll(cs, d, 0), 0.0)
            d *= 2
        p_incl = jnp.exp(cs)
        p_inv = jnp.exp(-cs)
        p_prev = jnp.exp(cs - lw)
        rt = (r_ref[0, rows, :] * p_incl).astype(BF16)
        nt = (kk_ref[0, rows, :] * (-p_prev)).astype(BF16)
        bt = (bb_ref[0, rows, :] * p_inv).astype(BF16)
        kt = (k_ref[0, rows, :] * p_inv).astype(BF16)
        vv = v_ref[0, rows, :].astype(BF16)
        sub.update(p_last=p_incl[C - 1:C, :], bt2=[bt[:, cl] for cl in cols], kt2=[kt[:, cl] for cl in cols],
                   vv2=[vv[:, cl] for cl in cols])
        sub["nr"] = [jnp.concatenate([jnp.where(own[q], nt[:, cols[p]], zero16),
                                      jnp.where(own[q], rt[:, cols[p]], zero16)], axis=0) for p, q in hp]
        g = [_dot_nt(sub["nr"][i], jnp.concatenate([sub["bt2"][p], sub["kt2"][p]], axis=0))
             for i, (p, q) in enumerate(hp)]
        yield
        gt = [jnp.where(strict2, gi[:C], 0.0) for gi in g]
        a_k = [pltpu.roll(gti, HEAD, 1)[:, :HEAD].astype(BF16) for gti in gt]
        sub["av"] = [_dot(a_k[i], sub["vv2"][p]) for i, (p, q) in enumerate(hp)]
        z = [jnp.where(even_lanes, gti, eye_hi) for gti in gt]
        sub["lo"] = [jnp.where(incl2, gi[C:], 0.0).astype(BF16) for gi in g]
        for _ in range(n_stage):
            yield
            zb = [zi.astype(BF16) for zi in z]
            res = [_dot(zbi[:, :HEAD], zbi) for zbi in zb]
            z = [jnp.where(even_lanes, ri, zi + ri) for zi, ri in zip(z, res)]
        sub["t16"] = [pltpu.roll(zi, HEAD, 1)[:, :HEAD].astype(BF16) for zi in z]

    def phase_b(ci, sub, carry):
        rows = slice(ci * C, (ci + 1) * C)
        state = carry["state"]
        s16 = [s.astype(BF16) for s in state]
        x0 = [_dot_nt(sub["nr"][i], jnp.concatenate([s16[p], zero_s] if q == 0 else [zero_s, s16[p]], axis=0))
              for i, (p, q) in enumerate(hp)]
        yield
        w16 = [(x0[i][:C] + sub["av"][i]).astype(BF16) for i in range(len(hp))]
        u16 = [_dot(sub["t16"][i], w16[i]).astype(BF16) for i in range(len(hp))]
        yield
        new_state = []
        for p in pairs:
            lhs = jnp.concatenate([u16[2 * p], u16[2 * p + 1], sub["vv2"][p]], axis=0)
            rhs = jnp.concatenate([jnp.where(own[0], sub["bt2"][p], zero16),
                                   jnp.where(own[1], sub["bt2"][p], zero16), sub["kt2"][p]], axis=0)
            upd = _dot_tn(lhs, rhs)
            upd = jnp.where(even_lanes, upd[:HEAD], upd[HEAD:])
            new_state.append((state[p] + upd) * sub["p_last"][:, cols[p]])
        carry["state"] = new_state
        yield
        outs = [x0[i][C:] + _dot(sub["lo"][i], jnp.concatenate([u16[i], sub["vv2"][p]], axis=0))
                for i, (p, q) in enumerate(hp)]
        for p in pairs:
            o_ref[0, rows, cols[p]] = jnp.where(even_lanes, outs[2 * p], outs[2 * p + 1])

    def run_interleaved(*gens):
        live = list(gens)
        while live:
            for gen in list(live):
                try:
                    next(gen)
                except StopIteration:
                    live.remove(gen)

    carry = dict(state=[s_scr[p] for p in pairs])
    subs = [dict() for _ in range(n_sub)]
    run_interleaved(*[phase_a(ci, subs[ci]) for ci in range(n_sub)])
    for ci in range(n_sub):
        run_interleaved(phase_b(ci, subs[ci], carry))
    for p in pairs:
        s_scr[p] = carry["state"][p]

    @pl.when(c == pl.num_programs(1) - 1)
    def _():
        s_out_ref[0] = s_scr[...]


def _wkv_chunked(r, k, v, lw, kk, bb, s0):
    B, T, D = r.shape
    heads = D // HEAD
    chunk = WKV_CHUNK
    assert chunk == HEAD and heads % 2 == 0
    Tp = -(-T // chunk) * chunk
    if Tp != T:
        r, k, v, lw, kk, bb = (jnp.pad(a, ((0, 0), (0, Tp - T), (0, 0))) for a in (r, k, v, lw, kk, bb))
    pair = lambda s: s.reshape(B, heads // 2, 2, HEAD, HEAD).transpose(0, 1, 3, 2, 4).reshape(
        B, heads // 2, HEAD, 2 * HEAD)
    unpair = lambda s: s.reshape(B, heads // 2, HEAD, 2, HEAD).transpose(0, 1, 3, 2, 4).reshape(
        B, heads, HEAD, HEAD)
    rows = chunk * WKV_SUB if Tp % (chunk * WKV_SUB) == 0 else chunk
    blk = pl.BlockSpec((1, rows, D), lambda b, c: (b, c, 0))
    st = pl.BlockSpec((1, heads // 2, HEAD, 2 * HEAD), lambda b, c: (b, 0, 0, 0))
    o, s_out = pl.pallas_call(
        functools.partial(_wkv_chunk_kernel, heads=heads),
        grid=(B, Tp // rows),
        in_specs=[blk] * 6 + [st],
        out_specs=[blk, st],
        out_shape=[jax.ShapeDtypeStruct((B, Tp, D), F32),
                   jax.ShapeDtypeStruct((B, heads // 2, HEAD, 2 * HEAD), F32)],
        scratch_shapes=[pltpu.VMEM((heads // 2, HEAD, 2 * HEAD), F32)],
        compiler_params=pltpu.CompilerParams(
            dimension_semantics=("parallel", "arbitrary"), vmem_limit_bytes=VMEM_LIMIT),
        name="wkv_chunk",
    )(r, k, v, lw, kk, bb, pair(s0))
    return o[:, :T], unpair(s_out)


def _qkv_kernel(h_ref, gn_ref, w_ref, q_out, kp_out, vp_out, kl_out, vl_out, *, npad, last_from):
    i = pl.program_id(1)
    D = h_ref.shape[2]

    @pl.when(i < npad)
    def _():
        kp_out[0] = jnp.zeros(kp_out.shape[1:], kp_out.dtype)
        vp_out[0] = jnp.zeros(vp_out.shape[1:], vp_out.dtype)

    @pl.when(i >= npad)
    def _():
        xn = _rms(h_ref[0], gn_ref[...]).astype(BF16)
        q = _dot(xn, w_ref[:, 0:D])
        k = _dot(xn, w_ref[:, D:2 * D])
        v = _dot(xn, w_ref[:, 2 * D:3 * D])
        q_out[0] = (q * (HEAD ** -0.5 * LOG2E)).astype(BF16)
        kp_out[0] = k.astype(BF16)
        vp_out[0] = v.astype(BF16)

        @pl.when(i >= npad + last_from)
        def _():
            kl_out[0] = k
            vl_out[0] = v


def _qkv_proj(h, gn, w_qkv, *, tq, pad_rows, keep):
    B, T, D = h.shape
    assert T % tq == 0 and pad_rows % tq == 0
    npad = pad_rows // tq
    assert keep % tq == 0
    last_from = (T - keep) // tq
    blk_in = pl.BlockSpec((1, tq, D), lambda b, i: (b, jnp.maximum(i - npad, 0), 0))
    blk_pad = pl.BlockSpec((1, tq, D), lambda b, i: (b, i, 0))
    blk_last = pl.BlockSpec((1, tq, D), lambda b, i: (b, jnp.maximum(i - npad - last_from, 0), 0))
    return pl.pallas_call(
        functools.partial(_qkv_kernel, npad=npad, last_from=last_from),
        grid=(B, npad + T // tq),
        in_specs=[blk_in, _const_spec(gn.shape), _const_spec(w_qkv.shape)],
        out_specs=[blk_in, blk_pad, blk_pad, blk_last, blk_last],
        out_shape=[jax.ShapeDtypeStruct((B, T, D), BF16),
                   jax.ShapeDtypeStruct((B, pad_rows + T, D), BF16),
                   jax.ShapeDtypeStruct((B, pad_rows + T, D), BF16),
                   jax.ShapeDtypeStruct((B, keep, D), F32),
                   jax.ShapeDtypeStruct((B, keep, D), F32)],
        compiler_params=pltpu.CompilerParams(
            dimension_semantics=("parallel", "arbitrary"), vmem_limit_bytes=VMEM_LIMIT),
        name="qkv_proj",
    )(h, gn, w_qkv)


def _band_attn_kernel(q_ref, k_ref, v_ref, tbl_ref, o_ref, *, nk, pad_rows):
    mq = q_ref.shape[1]
    W2 = 2 * HEAD
    n_cols = q_ref.shape[2] // W2
    c = pl.program_id(2)
    start = pl.multiple_of(c * mq, mq)
    col = lax.broadcasted_iota(jnp.int32, (1, nk), 1)
    pen = jnp.where(col >= pad_rows - c * mq, 0.0, NEG_INF)
    lane = lax.broadcasted_iota(jnp.int32, (mq, W2), 1)
    even_lanes = lane < HEAD
    own = [even_lanes, jnp.logical_not(even_lanes)]
    cols = [slice(p * W2, (p + 1) * W2) for p in range(n_cols)]
    hp = [(p, q) for p in range(n_cols) for q in range(2)]
    kb = [k_ref[0, pl.ds(start, nk), cl] for cl in cols]
    vb = [v_ref[0, pl.ds(start, nk), cl] for cl in cols]
    q2 = [q_ref[0, :, cl] for cl in cols]
    s = [_dot_nt(jnp.where(own[q], q2[p], jnp.zeros_like(q2[p])), kb[p]) + tbl_ref[i] + pen
         for i, (p, q) in enumerate(hp)]
    m = [jnp.max(si, axis=-1, keepdims=True) for si in s]
    pr = [jnp.exp2(si - mi) for si, mi in zip(s, m)]
    l = [jnp.sum(pi, axis=-1, keepdims=True) for pi in pr]
    o = [_dot(pr[i].astype(BF16), vb[p]) * (1.0 / l[i]) for i, (p, q) in enumerate(hp)]
    for p in range(n_cols):
        o_ref[0, :, cols[p]] = jnp.where(even_lanes, o[2 * p], o[2 * p + 1]).astype(o_ref.dtype)


def _band_attention(q, kpad, vpad, tbl, *, mq, nk, pad_rows, heads_per_step):
    B, T, D = q.shape
    Tk = kpad.shape[1]
    width = heads_per_step * HEAD
    assert T % mq == 0 and D % width == 0 and width % (2 * HEAD) == 0
    q_spec = pl.BlockSpec((1, mq, width), lambda g, b, c: (b, c, g))
    kv_spec = pl.BlockSpec((1, Tk, width), lambda g, b, c: (b, 0, g))
    tbl_spec = pl.BlockSpec((heads_per_step, mq, nk), lambda g, b, c: (g, 0, 0))
    return pl.pallas_call(
        functools.partial(_band_attn_kernel, nk=nk, pad_rows=pad_rows),
        grid=(D // width, B, T // mq),
        in_specs=[q_spec, kv_spec, kv_spec, tbl_spec],
        out_specs=q_spec,
        out_shape=jax.ShapeDtypeStruct((B, T, D), BF16),
        compiler_params=pltpu.CompilerParams(
            dimension_semantics=("parallel", "parallel", "arbitrary"), vmem_limit_bytes=VMEM_LIMIT),
        name="band_attn",
    )(q, kpad, vpad, tbl)


def _sample_attn_kernel(q_ref, kc_ref, vc_ref, kn_ref, vn_ref, tc_ref, tn_ref, o_ref, *, heads):
    W2 = 2 * HEAD
    tq = q_ref.shape[1]
    lane = lax.broadcasted_iota(jnp.int32, (tq, W2), 1)
    even_lanes = lane < HEAD
    own = [even_lanes, jnp.logical_not(even_lanes)]
    hp = [(p, q) for p in range(heads // 2) for q in range(2)]
    cols = [slice(p * W2, (p + 1) * W2) for p in range(heads // 2)]
    q2 = [q_ref[0, :, cl] for cl in cols]
    qm = [jnp.where(own[q], q2[p], jnp.zeros_like(q2[p])) for p, q in hp]
    s_c = [_dot_nt(qm[i], kc_ref[0, :, cols[p]]) + tc_ref[i] for i, (p, q) in enumerate(hp)]
    s_n = [_dot_nt(qm[i], kn_ref[0, :, cols[p]]) + tn_ref[i] for i, (p, q) in enumerate(hp)]
    m = [jnp.maximum(jnp.max(a, axis=-1, keepdims=True), jnp.max(b, axis=-1, keepdims=True))
         for a, b in zip(s_c, s_n)]
    p_c = [jnp.exp2(a - mi) for a, mi in zip(s_c, m)]
    p_n = [jnp.exp2(b - mi) for b, mi in zip(s_n, m)]
    l = [jnp.sum(a, axis=-1, keepdims=True) + jnp.sum(b, axis=-1, keepdims=True) for a, b in zip(p_c, p_n)]
    o = [(_dot(p_c[i].astype(BF16), vc_ref[0, :, cols[p]]) + _dot(p_n[i].astype(BF16), vn_ref[0, :, cols[p]]))
         * (1.0 / l[i]) for i, (p, q) in enumerate(hp)]
    for p in range(heads // 2):
        o_ref[0, :, cols[p]] = jnp.where(even_lanes, o[2 * p], o[2 * p + 1]).astype(o_ref.dtype)


def _sample_attention(q, kc, vc, kn, vn, tbl):
    B, T, D = q.shape
    W = kc.shape[1]
    heads = D // HEAD
    new = pl.BlockSpec((1, T, D), lambda b: (b, 0, 0))
    old = pl.BlockSpec((1, W, D), lambda b: (b, 0, 0))
    tc, tn = tbl[:, :, :W], tbl[:, :, W:]
    return pl.pallas_call(
        functools.partial(_sample_attn_kernel, heads=heads),
        grid=(B,),
        in_specs=[new, old, old, new, new, _const_spec(tc.shape), _const_spec(tn.shape)],
        out_specs=new,
        out_shape=jax.ShapeDtypeStruct((B, T, D), BF16),
        compiler_params=pltpu.CompilerParams(dimension_semantics=("parallel",), vmem_limit_bytes=VMEM_LIMIT),
        name="sample_attn",
    )(q, kc, vc, kn, vn, tc, tn)


def _bias_table(rel_bias, mq, nk, banded):
    period = nk + mq
    e = jnp.arange(period, dtype=jnp.int32)
    e = jnp.where(e <= nk, e, e - period)
    rel = jnp.clip(ATT_WINDOW - e, -(CHUNK - 1), REL_MAX_PAST) + (CHUNK - 1)
    line = rel_bias.astype(F32)[:, rel]
    tiled = jnp.tile(line, (1, mq))[:, :mq * (period - 1)]
    tbl = tiled.reshape(-1, mq, period - 1)[:, :, :nk]
    i = jnp.arange(mq, dtype=jnp.int32)[:, None]
    j = jnp.arange(nk, dtype=jnp.int32)[None, :]
    if banded:
        qc, kc = i // CHUNK, j // CHUNK
        allowed = (kc >= qc) & (kc <= qc + LEFT_CHUNKS)
        tbl = jnp.where(allowed[None], tbl, NEG_INF)
    return tbl * LOG2E


def _mix_ffn_kernel(*refs, rwkv, final, n_ff_chunks):
    it = iter(refs)
    h_ref = next(it)
    if rwkv:
        o_ref, bonus_ref, g_ref = next(it), next(it), next(it)
        lnw_ref, lnb_ref, seg_ref, exp_ref = next(it), next(it), next(it), next(it)
    else:
        a_ref = next(it)
    wo_ref, gf_ref, wgu_ref, wd_ref = next(it), next(it), next(it), next(it)
    gfin_ref = next(it) if final else None
    out_ref = next(it)

    if rwkv:
        o = o_ref[...]
        mean = _seg_expand(_seg_sum(o, seg_ref) * (1.0 / HEAD), exp_ref)
        d = o - mean
        rstd = lax.rsqrt(_seg_sum(d * d, seg_ref) * (1.0 / HEAD) + LNX_EPS)
        y = d * _seg_expand(rstd, exp_ref) * lnw_ref[...] + lnb_ref[...]
        a = ((y + bonus_ref[...]) * g_ref[...]).astype(BF16)
    else:
        a = a_ref[...]
    h1 = h_ref[...] + _dot(a, wo_ref[...])

    xn = _rms(h1, gf_ref[...]).astype(BF16)
    dff = wd_ref.shape[0]
    fc = dff // n_ff_chunks
    acc = h1
    for c in range(n_ff_chunks):
        gate = _dot(xn, wgu_ref[:, c * fc:(c + 1) * fc])
        up = _dot(xn, wgu_ref[:, dff + c * fc:dff + (c + 1) * fc])
        act = (gate * jax.nn.sigmoid(gate) * up).astype(BF16)
        acc = acc + _dot(act, wd_ref[c * fc:(c + 1) * fc, :])
    if final:
        acc = _rms(acc, gfin_ref[...])
    out_ref[...] = acc


def _mix_ffn(h, mix_inputs, mix_weights, w_out, gf, w_gu, w_down, g_final, *, rwkv, tm, n_ff_chunks):
    N, D = h.shape
    assert N % tm == 0
    final = g_final is not None
    row = pl.BlockSpec((tm, D), lambda i: (i, 0))
    consts = list(mix_weights) + [w_out, gf, w_gu, w_down] + ([g_final] if final else [])
    return pl.pallas_call(
        functools.partial(_mix_ffn_kernel, rwkv=rwkv, final=final, n_ff_chunks=n_ff_chunks),
        grid=(N // tm,),
        in_specs=[row] + [row] * len(mix_inputs) + [_const_spec(w.shape) for w in consts],
        out_specs=row,
        out_shape=jax.ShapeDtypeStruct((N, D), F32),
        compiler_params=pltpu.CompilerParams(
            dimension_semantics=("parallel",), vmem_limit_bytes=VMEM_LIMIT),
        name="mix_ffn_rwkv" if rwkv else "mix_ffn_attn",
    )(h, *mix_inputs, *consts)


def _row_block(n, want):
    t = min(want, n)
    while n % t:
        t //= 2
    return t


def kernel(x_prompt, x_sample, state_wkv, state_shift, cache_k, cache_v, norm_mix, norm_ffn, norm_final, rwkv_mu, rwkv_w_rkv, rwkv_w_out, rwkv_decay_w0, rwkv_decay_w1, rwkv_decay_w2, rwkv_iclr_a0, rwkv_iclr_a1, rwkv_iclr_a2, rwkv_vres_v0, rwkv_vres_v1, rwkv_vres_v2, rwkv_gate_g1, rwkv_gate_g2, rwkv_k_k, rwkv_k_a, rwkv_r_k, rwkv_lnx_w, rwkv_lnx_b, attn_w_qkv, attn_w_out, attn_rel_bias, ffn_w_gu, ffn_w_down):
    B, T, D = x_prompt.shape
    Bs, Ts, _ = x_sample.shape
    depth = norm_mix.shape[0]
    heads = D // HEAD
    W = cache_k.shape[2]
    assert W == ATT_WINDOW and Ts <= CHUNK and PAST_LEN % CHUNK == 0

    vec = lambda a: a.reshape(1, D).astype(F32)
    bf = lambda a: a.astype(BF16)

    col = jnp.arange(D, dtype=jnp.int32)[:, None] // HEAD
    seg = (col == jnp.arange(SEG_LANES, dtype=jnp.int32)[None, :]).astype(BF16)
    exp = jnp.concatenate([seg.T, seg.T], axis=0)

    hp, hs = x_prompt, x_sample
    vf_p = vf_s = None
    wkv_p, shift_p, k_p, v_p = [], [], [], []
    wkv_s, shift_s, k_s, v_s = [], [], [], []

    tm_proj = _row_block(T, 256)
    tm_ffn_p = _row_block(B * T, 256)
    tm_ffn_s = _row_block(Bs * Ts, 512)
    tq = _row_block(min(T, ATT_WINDOW), 512)
    group = _row_block(T, 4 * CHUNK)
    nk_p = ATT_WINDOW + group
    tbl_shapes = (group, nk_p), (Ts, W + Ts)

    for layer in range(depth):
        j = layer // 2
        gn = vec(norm_mix[layer])
        last = layer == depth - 1
        ffn_args = dict(gf=vec(norm_ffn[layer]), w_gu=bf(ffn_w_gu[layer]), w_down=bf(ffn_w_down[layer]),
                        g_final=vec(norm_final) if last else None, n_ff_chunks=2)
        if layer % 2 == 0:
            p = dict(mu=jnp.pad(rwkv_mu[j].astype(F32), ((0, 2), (0, 0))), w_rkv=bf(rwkv_w_rkv[j]),
                     w0=vec(rwkv_decay_w0[j]), w1=bf(rwkv_decay_w1[j]), w2=bf(rwkv_decay_w2[j]),
                     a0=vec(rwkv_iclr_a0[j]), a1=bf(rwkv_iclr_a1[j]), a2=bf(rwkv_iclr_a2[j]),
                     g1=bf(rwkv_gate_g1[j]), g2=bf(rwkv_gate_g2[j]),
                     k_k=vec(rwkv_k_k[j]), k_a=vec(rwkv_k_a[j]), r_k=vec(rwkv_r_k[j]))
            if j > 0:
                p.update(v0=vec(rwkv_vres_v0[j - 1]), v1=bf(rwkv_vres_v1[j - 1]), v2=bf(rwkv_vres_v2[j - 1]))
            lnw, lnb, w_out = vec(rwkv_lnx_w[j]), vec(rwkv_lnx_b[j]), bf(rwkv_w_out[j])

            def rwkv_side(h, shift, s0, vfirst, tm_proj, tm_ffn):
                b_, t_, _ = h.shape
                (r, k, v, lw, kk, bb, g, bonus), x_last = _tmix_proj(
                    h, shift, vfirst, gn, p, seg, exp, tm=tm_proj)
                o, s_new = _wkv_chunked(r, k, v, lw, kk, bb, s0)
                rows = lambda a: a.reshape(b_ * t_, D)
                h_new = _mix_ffn(rows(h), (rows(o), rows(bonus), rows(g)), (lnw, lnb, seg, exp), w_out,
                                 rwkv=True, tm=tm_ffn, **ffn_args)
                return h_new.reshape(h.shape), x_last, s_new, (v if vfirst is None else vfirst)

            hp, sh, st, vf_p = rwkv_side(hp, jnp.zeros((B, D), F32), jnp.zeros((B, heads, HEAD, HEAD), F32),
                                         vf_p, tm_proj, tm_ffn_p)
            wkv_p.append(st); shift_p.append(sh)
            hs, sh, st, vf_s = rwkv_side(hs, state_shift[j], state_wkv[j].astype(F32),
                                         vf_s, tm_ffn_s, tm_ffn_s)
            wkv_s.append(st); shift_s.append(sh)
        else:
            w_qkv, w_out = bf(attn_w_qkv[j]), bf(attn_w_out[j])
            q, kpad, vpad, k_last, v_last = _qkv_proj(hp, gn, w_qkv, tq=tq, pad_rows=ATT_WINDOW,
                                                      keep=min(ATT_WINDOW, T))
            tbl = _bias_table(attn_rel_bias[j], *tbl_shapes[0], banded=True)
            att = _band_attention(q, kpad, vpad, tbl, mq=group, nk=nk_p, pad_rows=ATT_WINDOW, heads_per_step=8)
            hp = _mix_ffn(hp.reshape(B * T, D), (att.reshape(B * T, D),), (), w_out,
                          rwkv=False, tm=tm_ffn_p, **ffn_args).reshape(B, T, D)
            k_p.append(k_last.reshape(B, -1, heads, HEAD)); v_p.append(v_last.reshape(B, -1, heads, HEAD))
            ns = Bs * Ts
            q, k16, v16, k_new, v_new = _qkv_proj(hs.reshape(1, ns, D), gn, w_qkv, tq=tm_ffn_s, pad_rows=0, keep=ns)
            tbl = _bias_table(attn_rel_bias[j], *tbl_shapes[1], banded=False)
            att = _sample_attention(q.reshape(Bs, Ts, D), bf(cache_k[j]).reshape(Bs, W, D),
                                    bf(cache_v[j]).reshape(Bs, W, D), k16.reshape(Bs, Ts, D),
                                    v16.reshape(Bs, Ts, D), tbl)
            hs = _mix_ffn(hs.reshape(ns, D), (att.reshape(ns, D),), (), w_out,
                          rwkv=False, tm=tm_ffn_s, **ffn_args).reshape(Bs, Ts, D)
            k_s.append(k_new.reshape(Bs, Ts, heads, HEAD)); v_s.append(v_new.reshape(Bs, Ts, heads, HEAD))

    return (hp, hs,
            jnp.stack(wkv_p), jnp.stack(shift_p), jnp.stack(k_p), jnp.stack(v_p),
            jnp.stack(wkv_s), jnp.stack(shift_s), jnp.stack(k_s), jnp.stack(v_s))
```

```python
import functools

import jax
import jax.numpy as jnp
from jax import lax
from jax.experimental import pallas as pl
from jax.experimental.pallas import tpu as pltpu

F32 = jnp.float32
BF16 = jnp.bfloat16

HEAD = 64
CHUNK = 64
LEFT_CHUNKS = 8
ATT_WINDOW = LEFT_CHUNKS * CHUNK
REL_MAX_PAST = 256
PAST_LEN = 4096
RMS_EPS = 1e-6
LNX_EPS = 64e-5
NEG_INF = -1e30
LOG2E = 1.4426950408889634
WKV_CHUNK = 64
WKV_SUB = 2
SEG_LANES = 128
VMEM_LIMIT = 56 * 1024 * 1024


def _dot(a, b):
    return jnp.dot(a, b, preferred_element_type=F32)


def _dot_nt(a, b):
    return lax.dot_general(a, b, (((1,), (1,)), ((), ())), preferred_element_type=F32)


def _dot_tn(a, b):
    return lax.dot_general(a, b, (((0,), (0,)), ((), ())), preferred_element_type=F32)


def _rms(x, g):
    return x * lax.rsqrt(jnp.mean(x * x, axis=-1, keepdims=True) + RMS_EPS) * g


def _const_spec(shape):
    nd = len(shape)
    return pl.BlockSpec(shape, lambda *_: (0,) * nd, pipeline_mode=pl.Buffered(1))


def _seg_sum(x, seg_ref):
    return _dot(x.astype(BF16), seg_ref[...])


def _seg_expand(s, exp_ref):
    hi = s.astype(BF16)
    lo = (s - hi.astype(F32)).astype(BF16)
    return _dot(jnp.concatenate([hi, lo], axis=-1), exp_ref[...])


def _tmix_proj_kernel(*refs, rows_mode, has_vres, seq_blocks):
    it = iter(refs)
    h_ref = next(it)
    bnd_ref = next(it)
    shift_ref = None if rows_mode else next(it)
    vfirst_ref = next(it) if has_vres else None
    gn_ref = next(it)
    mu_ref = next(it)
    wrkv_ref = next(it)
    w0_ref, w1_ref, w2_ref = next(it), next(it), next(it)
    a0_ref, a1_ref, a2_ref = next(it), next(it), next(it)
    if has_vres:
        v0_ref, v1_ref, v2_ref = next(it), next(it), next(it)
    g1_ref, g2_ref = next(it), next(it)
    kk_ref, ka_ref, rk_ref = next(it), next(it), next(it)
    seg_ref, exp_ref = next(it), next(it)
    (r_out, k_out, v_out, lw_out, kk_out, bb_out, g_out, bonus_out, xl_out) = it

    gn = gn_ref[...]
    x = _rms(h_ref[0], gn)
    tm = x.shape[0]
    row = lax.broadcasted_iota(jnp.int32, x.shape, 0)
    rolled = pltpu.roll(x, 1, 0)
    if rows_mode:
        seq_len = seq_blocks
        x_prev = jnp.where(row % seq_len == 0, bnd_ref[0], rolled)
        xl_out[0] = x
    else:
        halo = bnd_ref[0]
        prev_row = _rms(halo[7:8, :], gn)
        first = pl.program_id(1) % seq_blocks == 0
        prev_row = jnp.where(first, shift_ref[0], prev_row)
        x_prev = jnp.where(row == 0, prev_row, rolled)
        xl_out[0] = x[tm - 1:tm, :]
    xx = x_prev - x

    def mixed(j):
        return (x + xx * mu_ref[j:j + 1, :]).astype(BF16)

    m_v = mixed(2)
    r = _dot(mixed(0), wrkv_ref[0])
    k = _dot(mixed(1), wrkv_ref[1])
    v = _dot(m_v, wrkv_ref[2])

    w_pre = w0_ref[...] + _dot(jnp.tanh(_dot(mixed(3), w1_ref[...])).astype(BF16), w2_ref[...])
    lw = jax.nn.sigmoid(w_pre) * (-0.6065306597126334)
    a = jax.nn.sigmoid(a0_ref[...] + _dot(_dot(mixed(4), a1_ref[...]).astype(BF16), a2_ref[...]))
    g = _dot(jax.nn.sigmoid(_dot(mixed(5), g1_ref[...])).astype(BF16), g2_ref[...])
    if has_vres:
        gate = jax.nn.sigmoid(v0_ref[...] + _dot(_dot(m_v, v1_ref[...]).astype(BF16), v2_ref[...]))
        v = v + (vfirst_ref[0] - v) * gate

    kk = k * kk_ref[...]
    norm = jnp.maximum(jnp.sqrt(_seg_sum(kk * kk, seg_ref)), 1e-12)
    kk = kk * _seg_expand(1.0 / norm, exp_ref)
    k = k * (1.0 + (a - 1.0) * ka_ref[...])
    bonus = _seg_expand(_seg_sum(r * k * rk_ref[...], seg_ref), exp_ref) * v

    r_out[0] = r
    k_out[0] = k
    v_out[0] = v
    lw_out[0] = lw
    kk_out[0] = kk
    bb_out[0] = kk * a
    g_out[0] = g
    bonus_out[0] = bonus


def _tmix_proj(h, shift, vfirst, gn, p, seg, exp, *, tm):
    B, T, D = h.shape
    rows_mode = T < tm
    has_vres = vfirst is not None
    if rows_mode:
        nseq = tm // T
        assert (B * T) % tm == 0
        hb = h.reshape(B * T // tm, tm, D)
        bnd = jnp.zeros((B, T, D), F32).at[:, 0, :].set(shift).reshape(hb.shape)
        grid = (hb.shape[0], 1)
        blk = pl.BlockSpec((1, tm, D), lambda b, i: (b, 0, 0))
        in_arrays = [hb, bnd]
        in_specs = [blk, blk]
        seq_blocks = T
        if has_vres:
            in_arrays.append(vfirst.reshape(hb.shape))
            in_specs.append(blk)
        xl_shape = jax.ShapeDtypeStruct(hb.shape, F32)
        xl_spec = blk
        out_shape_main = hb.shape
    else:
        assert T % tm == 0 and tm % 8 == 0
        nblk = T // tm
        grid = (B, nblk)
        blk = pl.BlockSpec((1, tm, D), lambda b, i: (b, i, 0))
        halo = pl.BlockSpec((1, 8, D), lambda b, i: (b, jnp.maximum(i * (tm // 8) - 1, 0), 0))
        in_arrays = [h, h, shift.reshape(B, 1, D)]
        in_specs = [blk, halo, pl.BlockSpec((1, 1, D), lambda b, i: (b, 0, 0))]
        seq_blocks = nblk
        if has_vres:
            in_arrays.append(vfirst)
            in_specs.append(blk)
        xl_shape = jax.ShapeDtypeStruct((B, 1, D), F32)
        xl_spec = pl.BlockSpec((1, 1, D), lambda b, i: (b, 0, 0))
        out_shape_main = h.shape

    weights = [gn, p["mu"], p["w_rkv"], p["w0"], p["w1"], p["w2"], p["a0"], p["a1"], p["a2"]]
    if has_vres:
        weights += [p["v0"], p["v1"], p["v2"]]
    weights += [p["g1"], p["g2"], p["k_k"], p["k_a"], p["r_k"], seg, exp]
    in_arrays += weights
    in_specs += [_const_spec(w.shape) for w in weights]

    main = jax.ShapeDtypeStruct(out_shape_main, F32)
    outs = pl.pallas_call(
        functools.partial(_tmix_proj_kernel, rows_mode=rows_mode, has_vres=has_vres, seq_blocks=seq_blocks),
        grid=grid,
        in_specs=in_specs,
        out_specs=[blk] * 8 + [xl_spec],
        out_shape=[main] * 8 + [xl_shape],
        compiler_params=pltpu.CompilerParams(
            dimension_semantics=("parallel", "arbitrary"), vmem_limit_bytes=VMEM_LIMIT),
        name="tmix_proj",
    )(*in_arrays)
    main_outs = [o.reshape(B, T, D) for o in outs[:8]]
    if rows_mode:
        x_last = outs[8].reshape(B, T, D)[:, T - 1, :]
    else:
        x_last = outs[8].reshape(B, D)
    return main_outs, x_last


def _wkv_chunk_kernel(r_ref, k_ref, v_ref, lw_ref, kk_ref, bb_ref, s0_ref, o_ref, s_out_ref, s_scr, *, heads):
    c = pl.program_id(1)
    C = HEAD
    n_sub = r_ref.shape[1] // C
    W2 = 2 * HEAD

    @pl.when(c == 0)
    def _():
        s_scr[...] = s0_ref[0]

    lane = lax.broadcasted_iota(jnp.int32, (C, W2), 1)
    trow2 = lax.broadcasted_iota(jnp.int32, (C, W2), 0)
    s_idx = lane & (HEAD - 1)
    strict2 = trow2 > s_idx
    incl2 = trow2 >= s_idx
    even_lanes = lane < HEAD
    own = [even_lanes, jnp.logical_not(even_lanes)]
    eye_hi = jnp.where(trow2 == lane - HEAD, 1.0, 0.0)
    zero16 = jnp.zeros((C, W2), BF16)
    pairs = range(heads // 2)
    cols = [slice(p * W2, (p + 1) * W2) for p in pairs]
    hp = [(p, q) for p in pairs for q in range(2)]
    trow = lax.broadcasted_iota(jnp.int32, (C, r_ref.shape[2]), 0)

    n_stage = C.bit_length() - 1
    zero_s = jnp.zeros((HEAD, W2), BF16)

    def phase_a(ci, sub):
        rows = slice(ci * C, (ci + 1) * C)
        lw = lw_ref[0, rows, :]
        cs = lw
        d = 1
        while d < C:
            cs = cs + jnp.where(trow >= d, pltpu.roll(cs, d, 0), 0.0)
            d *= 2
        p_incl = jnp.exp(cs)
        p_inv = jnp.exp(-cs)
        p_prev = jnp.exp(cs - lw)
        rt = (r_ref[0, rows, :] * p_incl).astype(BF16)
        nt = (kk_ref[0, rows, :] * (-p_prev)).astype(BF16)
        bt = (bb_ref[0, rows, :] * p_inv).astype(BF16)
        kt = (k_ref[0, rows, :] * p_inv).astype(BF16)
        vv = v_ref[0, rows, :].astype(BF16)
        sub.update(p_last=p_incl[C - 1:C, :], bt2=[bt[:, cl] for cl in cols], kt2=[kt[:, cl] for cl in cols],
                   vv2=[vv[:, cl] for cl in cols])
        sub["nr"] = [jnp.concatenate([jnp.where(own[q], nt[:, cols[p]], zero16),
                                      jnp.where(own[q], rt[:, cols[p]], zero16)], axis=0) for p, q in hp]
        g = [_dot_nt(sub["nr"][i], jnp.concatenate([sub["bt2"][p], sub["kt2"][p]], axis=0))
             for i, (p, q) in enumerate(hp)]
        yield
        gt = [jnp.where(strict2, gi[:C], 0.0) for gi in g]
        a_k = [pltpu.roll(gti, HEAD, 1)[:, :HEAD].astype(BF16) for gti in gt]
        sub["av"] = [_dot(a_k[i], sub["vv2"][p]) for i, (p, q) in enumerate(hp)]
        z = [jnp.where(even_lanes, gti, eye_hi) for gti in gt]
        sub["lo"] = [jnp.where(incl2, gi[C:], 0.0).astype(BF16) for gi in g]
        for _ in range(n_stage):
            yield
            zb = [zi.astype(BF16) for zi in z]
            res = [_dot(zbi[:, :HEAD], zbi) for zbi in zb]
            z = [jnp.where(even_lanes, ri, zi + ri) for zi, ri in zip(z, res)]
        sub["t16"] = [pltpu.roll(zi, HEAD, 1)[:, :HEAD].astype(BF16) for zi in z]

    def phase_b(ci, sub, carry):
        rows = slice(ci * C, (ci + 1) * C)
        state = carry["state"]
        s16 = [s.astype(BF16) for s in state]
        x0 = [_dot_nt(sub["nr"][i], jnp.concatenate([s16[p], zero_s] if q == 0 else [zero_s, s16[p]], axis=0))
              for i, (p, q) in enumerate(hp)]
        yield
        w16 = [(x0[i][:C] + sub["av"][i]).astype(BF16) for i in range(len(hp))]
        u16 = [_dot(sub["t16"][i], w16[i]).astype(BF16) for i in range(len(hp))]
        yield
        new_state = []
        for p in pairs:
            lhs = jnp.concatenate([u16[2 * p], u16[2 * p + 1], sub["vv2"][p]], axis=0)
            rhs = jnp.concatenate([jnp.where(own[0], sub["bt2"][p], zero16),
                                   jnp.where(own[1], sub["bt2"][p], zero16), sub["kt2"][p]], axis=0)
            upd = _dot_tn(lhs, rhs)
            upd = jnp.where(even_lanes, upd[:HEAD], upd[HEAD:])
            new_state.append((state[p] + upd) * sub["p_last"][:, cols[p]])
        carry["state"] = new_state
        yield
        outs = [x0[i][C:] + _dot(sub["lo"][i], jnp.concatenate([u16[i], sub["vv2"][p]], axis=0))
                for i, (p, q) in enumerate(hp)]
        for p in pairs:
            o_ref[0, rows, cols[p]] = jnp.where(even_lanes, outs[2 * p], outs[2 * p + 1])

    def run_interleaved(*gens):
        live = list(gens)
        while live:
            for gen in list(live):
                try:
                    next(gen)
                except StopIteration:
                    live.remove(gen)

    carry = dict(state=[s_scr[p] for p in pairs])
    subs = [dict() for _ in range(n_sub)]
    run_interleaved(*[phase_a(ci, subs[ci]) for ci in range(n_sub)])
    for ci in range(n_sub):
        run_interleaved(phase_b(ci, subs[ci], carry))
    for p in pairs:
        s_scr[p] = carry["state"][p]

    @pl.when(c == pl.num_programs(1) - 1)
    def _():
        s_out_ref[0] = s_scr[...]


def _wkv_chunked(r, k, v, lw, kk, bb, s0):
    B, T, D = r.shape
    heads = D // HEAD
    chunk = WKV_CHUNK
    assert chunk == HEAD and heads % 2 == 0
    Tp = -(-T // chunk) * chunk
    if Tp != T:
        r, k, v, lw, kk, bb = (jnp.pad(a, ((0, 0), (0, Tp - T), (0, 0))) for a in (r, k, v, lw, kk, bb))
    pair = lambda s: s.reshape(B, heads // 2, 2, HEAD, HEAD).transpose(0, 1, 3, 2, 4).reshape(
        B, heads // 2, HEAD, 2 * HEAD)
    unpair = lambda s: s.reshape(B, heads // 2, HEAD, 2, HEAD).transpose(0, 1, 3, 2, 4).reshape(
        B, heads, HEAD, HEAD)
    rows = chunk * WKV_SUB if Tp % (chunk * WKV_SUB) == 0 else chunk
    blk = pl.BlockSpec((1, rows, D), lambda b, c: (b, c, 0))
    st = pl.BlockSpec((1, heads // 2, HEAD, 2 * HEAD), lambda b, c: (b, 0, 0, 0))
    o, s_out = pl.pallas_call(
        functools.partial(_wkv_chunk_kernel, heads=heads),
        grid=(B, Tp // rows),
        in_specs=[blk] * 6 + [st],
        out_specs=[blk, st],
        out_shape=[jax.ShapeDtypeStruct((B, Tp, D), F32),
                   jax.ShapeDtypeStruct((B, heads // 2, HEAD, 2 * HEAD), F32)],
        scratch_shapes=[pltpu.VMEM((heads // 2, HEAD, 2 * HEAD), F32)],
        compiler_params=pltpu.CompilerParams(
            dimension_semantics=("parallel", "arbitrary"), vmem_limit_bytes=VMEM_LIMIT),
        name="wkv_chunk",
    )(r, k, v, lw, kk, bb, pair(s0))
    return o[:, :T], unpair(s_out)


def _qkv_kernel(h_ref, gn_ref, w_ref, q_out, kp_out, vp_out, kl_out, vl_out, *, npad, last_from):
    i = pl.program_id(1)
    D = h_ref.shape[2]

    @pl.when(i < npad)
    def _():
        kp_out[0] = jnp.zeros(kp_out.shape[1:], kp_out.dtype)
        vp_out[0] = jnp.zeros(vp_out.shape[1:], vp_out.dtype)

    @pl.when(i >= npad)
    def _():
        xn = _rms(h_ref[0], gn_ref[...]).astype(BF16)
        q = _dot(xn, w_ref[:, 0:D])
        k = _dot(xn, w_ref[:, D:2 * D])
        v = _dot(xn, w_ref[:, 2 * D:3 * D])
        q_out[0] = (q * (HEAD ** -0.5 * LOG2E)).astype(BF16)
        kp_out[0] = k.astype(BF16)
        vp_out[0] = v.astype(BF16)

        @pl.when(i >= npad + last_from)
        def _():
            kl_out[0] = k
            vl_out[0] = v


def _qkv_proj(h, gn, w_qkv, *, tq, pad_rows, keep):
    B, T, D = h.shape
    assert T % tq == 0 and pad_rows % tq == 0
    npad = pad_rows // tq
    assert keep % tq == 0
    last_from = (T - keep) // tq
    blk_in = pl.BlockSpec((1, tq, D), lambda b, i: (b, jnp.maximum(i - npad, 0), 0))
    blk_pad = pl.BlockSpec((1, tq, D), lambda b, i: (b, i, 0))
    blk_last = pl.BlockSpec((1, tq, D), lambda b, i: (b, jnp.maximum(i - npad - last_from, 0), 0))
    return pl.pallas_call(
        functools.partial(_qkv_kernel, npad=npad, last_from=last_from),
        grid=(B, npad + T // tq),
        in_specs=[blk_in, _const_spec(gn.shape), _const_spec(w_qkv.shape)],
        out_specs=[blk_in, blk_pad, blk_pad, blk_last, blk_last],
        out_shape=[jax.ShapeDtypeStruct((B, T, D), BF16),
                   jax.ShapeDtypeStruct((B, pad_rows + T, D), BF16),
                   jax.ShapeDtypeStruct((B, pad_rows + T, D), BF16),
                   jax.ShapeDtypeStruct((B, keep, D), F32),
                   jax.ShapeDtypeStruct((B, keep, D), F32)],
        compiler_params=pltpu.CompilerParams(
            dimension_semantics=("parallel", "arbitrary"), vmem_limit_bytes=VMEM_LIMIT),
        name="qkv_proj",
    )(h, gn, w_qkv)


def _toeplitz(line, rows):
    return pltpu.roll(jnp.broadcast_to(line, (rows, line.shape[1])), 0, 1, stride=1, stride_axis=0)


def _band_attn_kernel(q_ref, k_ref, v_ref, line_ref, o_ref, tbl_ref, *, nk, pad_rows):
    mq = q_ref.shape[1]
    W2 = 2 * HEAD
    n_cols = q_ref.shape[2] // W2
    c = pl.program_id(2)

    @pl.when((pl.program_id(1) == 0) & (c == 0))
    def _():
        qc = lax.broadcasted_iota(jnp.int32, (mq, nk), 0) // CHUNK
        kc = lax.broadcasted_iota(jnp.int32, (mq, nk), 1) // CHUNK
        in_band = (kc >= qc) & (kc <= qc + LEFT_CHUNKS)
        for i in range(tbl_ref.shape[0]):
            tbl_ref[i] = jnp.where(in_band, _toeplitz(line_ref[i], mq)[:, :nk], NEG_INF)

    start = pl.multiple_of(c * mq, mq)
    col = lax.broadcasted_iota(jnp.int32, (1, nk), 1)
    pen = jnp.where(col >= pad_rows - c * mq, 0.0, NEG_INF)
    lane = lax.broadcasted_iota(jnp.int32, (mq, W2), 1)
    even_lanes = lane < HEAD
    own = [even_lanes, jnp.logical_not(even_lanes)]
    cols = [slice(p * W2, (p + 1) * W2) for p in range(n_cols)]
    hp = [(p, q) for p in range(n_cols) for q in range(2)]
    kb = [k_ref[0, pl.ds(start, nk), cl] for cl in cols]
    vb = [v_ref[0, pl.ds(start, nk), cl] for cl in cols]
    q2 = [q_ref[0, :, cl] for cl in cols]
    s = [_dot_nt(jnp.where(own[q], q2[p], jnp.zeros_like(q2[p])), kb[p]) + tbl_ref[i] + pen
         for i, (p, q) in enumerate(hp)]
    m = [jnp.max(si, axis=-1, keepdims=True) for si in s]
    pr = [jnp.exp2(si - mi) for si, mi in zip(s, m)]
    l = [jnp.sum(pi, axis=-1, keepdims=True) for pi in pr]
    o = [_dot(pr[i].astype(BF16), vb[p]) * (1.0 / l[i]) for i, (p, q) in enumerate(hp)]
    for p in range(n_cols):
        o_ref[0, :, cols[p]] = jnp.where(even_lanes, o[2 * p], o[2 * p + 1]).astype(o_ref.dtype)


def _band_attention(q, kpad, vpad, lines, *, mq, nk, pad_rows, heads_per_step):
    B, T, D = q.shape
    Tk = kpad.shape[1]
    width = heads_per_step * HEAD
    assert T % mq == 0 and D % width == 0 and width % (2 * HEAD) == 0
    q_spec = pl.BlockSpec((1, mq, width), lambda g, b, c: (b, c, g))
    kv_spec = pl.BlockSpec((1, Tk, width), lambda g, b, c: (b, 0, g))
    line_spec = pl.BlockSpec((heads_per_step, 1, lines.shape[2]), lambda g, b, c: (g, 0, 0))
    return pl.pallas_call(
        functools.partial(_band_attn_kernel, nk=nk, pad_rows=pad_rows),
        grid=(D // width, B, T // mq),
        in_specs=[q_spec, kv_spec, kv_spec, line_spec],
        out_specs=q_spec,
        out_shape=jax.ShapeDtypeStruct((B, T, D), BF16),
        scratch_shapes=[pltpu.VMEM((heads_per_step, mq, nk), F32)],
        compiler_params=pltpu.CompilerParams(
            dimension_semantics=("arbitrary", "arbitrary", "arbitrary"), vmem_limit_bytes=VMEM_LIMIT),
        name="band_attn",
    )(q, kpad, vpad, lines)


def _sample_attn_kernel(q_ref, kc_ref, vc_ref, kn_ref, vn_ref, line_ref, o_ref, tc_ref, tn_ref, *, heads):
    W2 = 2 * HEAD
    tq = q_ref.shape[1]
    nc = kc_ref.shape[1]

    @pl.when(pl.program_id(0) == 0)
    def _():
        for i in range(heads):
            t = _toeplitz(line_ref[i], tq)
            tc_ref[i] = t[:, :nc]
            tn_ref[i] = t[:, nc:nc + tq]

    lane = lax.broadcasted_iota(jnp.int32, (tq, W2), 1)
    even_lanes = lane < HEAD
    own = [even_lanes, jnp.logical_not(even_lanes)]
    hp = [(p, q) for p in range(heads // 2) for q in range(2)]
    cols = [slice(p * W2, (p + 1) * W2) for p in range(heads // 2)]
    q2 = [q_ref[0, :, cl] for cl in cols]
    qm = [jnp.where(own[q], q2[p], jnp.zeros_like(q2[p])) for p, q in hp]
    s_c = [_dot_nt(qm[i], kc_ref[0, :, cols[p]]) + tc_ref[i] for i, (p, q) in enumerate(hp)]
    s_n = [_dot_nt(qm[i], kn_ref[0, :, cols[p]]) + tn_ref[i] for i, (p, q) in enumerate(hp)]
    m = [jnp.maximum(jnp.max(a, axis=-1, keepdims=True), jnp.max(b, axis=-1, keepdims=True))
         for a, b in zip(s_c, s_n)]
    p_c = [jnp.exp2(a - mi) for a, mi in zip(s_c, m)]
    p_n = [jnp.exp2(b - mi) for b, mi in zip(s_n, m)]
    l = [jnp.sum(a, axis=-1, keepdims=True) + jnp.sum(b, axis=-1, keepdims=True) for a, b in zip(p_c, p_n)]
    o = [(_dot(p_c[i].astype(BF16), vc_ref[0, :, cols[p]]) + _dot(p_n[i].astype(BF16), vn_ref[0, :, cols[p]]))
         * (1.0 / l[i]) for i, (p, q) in enumerate(hp)]
    for p in range(heads // 2):
        o_ref[0, :, cols[p]] = jnp.where(even_lanes, o[2 * p], o[2 * p + 1]).astype(o_ref.dtype)


def _sample_attention(q, kc, vc, kn, vn, lines):
    B, T, D = q.shape
    W = kc.shape[1]
    heads = D // HEAD
    new = pl.BlockSpec((1, T, D), lambda b: (b, 0, 0))
    old = pl.BlockSpec((1, W, D), lambda b: (b, 0, 0))
    return pl.pallas_call(
        functools.partial(_sample_attn_kernel, heads=heads),
        grid=(B,),
        in_specs=[new, old, old, new, new, _const_spec(lines.shape)],
        out_specs=new,
        out_shape=jax.ShapeDtypeStruct((B, T, D), BF16),
        scratch_shapes=[pltpu.VMEM((heads, T, W), F32), pltpu.VMEM((heads, T, T), F32)],
        compiler_params=pltpu.CompilerParams(dimension_semantics=("arbitrary",), vmem_limit_bytes=VMEM_LIMIT),
        name="sample_attn",
    )(q, kc, vc, kn, vn, lines)


def _bias_lines(rel_bias, mq, nk):
    period = -(-(nk + mq) // 128) * 128
    e = jnp.arange(period, dtype=jnp.int32)
    e = jnp.where(e < nk, e, e - period)
    rel = jnp.clip(ATT_WINDOW - e, -(CHUNK - 1), REL_MAX_PAST) + (CHUNK - 1)
    return (rel_bias.astype(F32)[:, rel] * LOG2E)[:, None, :]


def _mix_ffn_kernel(*refs, rwkv, final, n_ff_chunks):
    it = iter(refs)
    h_ref = next(it)
    if rwkv:
        o_ref, bonus_ref, g_ref = next(it), next(it), next(it)
        lnw_ref, lnb_ref, seg_ref, exp_ref = next(it), next(it), next(it), next(it)
    else:
        a_ref = next(it)
    wo_ref, gf_ref, wgu_ref, wd_ref = next(it), next(it), next(it), next(it)
    gfin_ref = next(it) if final else None
    out_ref = next(it)

    if rwkv:
        o = o_ref[...]
        mean = _seg_expand(_seg_sum(o, seg_ref) * (1.0 / HEAD), exp_ref)
        d = o - mean
        rstd = lax.rsqrt(_seg_sum(d * d, seg_ref) * (1.0 / HEAD) + LNX_EPS)
        y = d * _seg_expand(rstd, exp_ref) * lnw_ref[...] + lnb_ref[...]
        a = ((y + bonus_ref[...]) * g_ref[...]).astype(BF16)
    else:
        a = a_ref[...]
    h1 = h_ref[...] + _dot(a, wo_ref[...])

    xn = _rms(h1, gf_ref[...]).astype(BF16)
    dff = wd_ref.shape[0]
    fc = dff // n_ff_chunks
    acc = h1
    for c in range(n_ff_chunks):
        gate = _dot(xn, wgu_ref[:, c * fc:(c + 1) * fc])
        up = _dot(xn, wgu_ref[:, dff + c * fc:dff + (c + 1) * fc])
        act = (gate * jax.nn.sigmoid(gate) * up).astype(BF16)
        acc = acc + _dot(act, wd_ref[c * fc:(c + 1) * fc, :])
    if final:
        acc = _rms(acc, gfin_ref[...])
    out_ref[...] = acc


def _mix_ffn(h, mix_inputs, mix_weights, w_out, gf, w_gu, w_down, g_final, *, rwkv, tm, n_ff_chunks):
    N, D = h.shape
    assert N % tm == 0
    final = g_final is not None
    row = pl.BlockSpec((tm, D), lambda i: (i, 0))
    consts = list(mix_weights) + [w_out, gf, w_gu, w_down] + ([g_final] if final else [])
    return pl.pallas_call(
        functools.partial(_mix_ffn_kernel, rwkv=rwkv, final=final, n_ff_chunks=n_ff_chunks),
        grid=(N // tm,),
        in_specs=[row] + [row] * len(mix_inputs) + [_const_spec(w.shape) for w in consts],
        out_specs=row,
        out_shape=jax.ShapeDtypeStruct((N, D), F32),
        compiler_params=pltpu.CompilerParams(
            dimension_semantics=("parallel",), vmem_limit_bytes=VMEM_LIMIT),
        name="mix_ffn_rwkv" if rwkv else "mix_ffn_attn",
    )(h, *mix_inputs, *consts)


def _row_block(n, want):
    t = min(want, n)
    while n % t:
        t //= 2
    return t


def kernel(x_prompt, x_sample, state_wkv, state_shift, cache_k, cache_v, norm_mix, norm_ffn, norm_final, rwkv_mu, rwkv_w_rkv, rwkv_w_out, rwkv_decay_w0, rwkv_decay_w1, rwkv_decay_w2, rwkv_iclr_a0, rwkv_iclr_a1, rwkv_iclr_a2, rwkv_vres_v0, rwkv_vres_v1, rwkv_vres_v2, rwkv_gate_g1, rwkv_gate_g2, rwkv_k_k, rwkv_k_a, rwkv_r_k, rwkv_lnx_w, rwkv_lnx_b, attn_w_qkv, attn_w_out, attn_rel_bias, ffn_w_gu, ffn_w_down):
    B, T, D = x_prompt.shape
    Bs, Ts, _ = x_sample.shape
    depth = norm_mix.shape[0]
    heads = D // HEAD
    W = cache_k.shape[2]
    assert W == ATT_WINDOW and Ts <= CHUNK and PAST_LEN % CHUNK == 0

    vec = lambda a: a.reshape(1, D).astype(F32)
    bf = lambda a: a.astype(BF16)

    col = jnp.arange(D, dtype=jnp.int32)[:, None] // HEAD
    seg = (col == jnp.arange(SEG_LANES, dtype=jnp.int32)[None, :]).astype(BF16)
    exp = jnp.concatenate([seg.T, seg.T], axis=0)

    hp, hs = x_prompt, x_sample
    vf_p = vf_s = None
    wkv_p, shift_p, k_p, v_p = [], [], [], []
    wkv_s, shift_s, k_s, v_s = [], [], [], []

    tm_proj = _row_block(T, 256)
    tm_ffn_p = _row_block(B * T, 256)
    tm_ffn_pa = _row_block(B * T, 512)
    tm_ffn_s = _row_block(Bs * Ts, 512)
    tq = _row_block(min(T, ATT_WINDOW), 512)
    group = _row_block(T, 4 * CHUNK)
    nk_p = ATT_WINDOW + group

    for layer in range(depth):
        j = layer // 2
        gn = vec(norm_mix[layer])
        last = layer == depth - 1
        ffn_args = dict(gf=vec(norm_ffn[layer]), w_gu=bf(ffn_w_gu[layer]), w_down=bf(ffn_w_down[layer]),
                        g_final=vec(norm_final) if last else None, n_ff_chunks=1)
        if layer % 2 == 0:
            p = dict(mu=jnp.pad(rwkv_mu[j].astype(F32), ((0, 2), (0, 0))), w_rkv=bf(rwkv_w_rkv[j]),
                     w0=vec(rwkv_decay_w0[j]), w1=bf(rwkv_decay_w1[j]), w2=bf(rwkv_decay_w2[j]),
                     a0=vec(rwkv_iclr_a0[j]), a1=bf(rwkv_iclr_a1[j]), a2=bf(rwkv_iclr_a2[j]),
                     g1=bf(rwkv_gate_g1[j]), g2=bf(rwkv_gate_g2[j]),
                     k_k=vec(rwkv_k_k[j]), k_a=vec(rwkv_k_a[j]), r_k=vec(rwkv_r_k[j]))
            if j > 0:
                p.update(v0=vec(rwkv_vres_v0[j - 1]), v1=bf(rwkv_vres_v1[j - 1]), v2=bf(rwkv_vres_v2[j - 1]))
            lnw, lnb, w_out = vec(rwkv_lnx_w[j]), vec(rwkv_lnx_b[j]), bf(rwkv_w_out[j])

            def rwkv_side(h, shift, s0, vfirst, tm_proj, tm_ffn):
                b_, t_, _ = h.shape
                (r, k, v, lw, kk, bb, g, bonus), x_last = _tmix_proj(
                    h, shift, vfirst, gn, p, seg, exp, tm=tm_proj)
                o, s_new = _wkv_chunked(r, k, v, lw, kk, bb, s0)
                rows = lambda a: a.reshape(b_ * t_, D)
                h_new = _mix_ffn(rows(h), (rows(o), rows(bonus), rows(g)), (lnw, lnb, seg, exp), w_out,
                                 rwkv=True, tm=tm_ffn, **ffn_args)
                return h_new.reshape(h.shape), x_last, s_new, (v if vfirst is None else vfirst)

            hp, sh, st, vf_p = rwkv_side(hp, jnp.zeros((B, D), F32), jnp.zeros((B, heads, HEAD, HEAD), F32),
                                         vf_p, tm_proj, tm_ffn_p)
            wkv_p.append(st); shift_p.append(sh)
            hs, sh, st, vf_s = rwkv_side(hs, state_shift[j], state_wkv[j].astype(F32),
                                         vf_s, tm_ffn_s, tm_ffn_s)
            wkv_s.append(st); shift_s.append(sh)
        else:
            w_qkv, w_out = bf(attn_w_qkv[j]), bf(attn_w_out[j])
            q, kpad, vpad, k_last, v_last = _qkv_proj(hp, gn, w_qkv, tq=tq, pad_rows=ATT_WINDOW,
                                                      keep=min(ATT_WINDOW, T))
            att = _band_attention(q, kpad, vpad, _bias_lines(attn_rel_bias[j], group, nk_p), mq=group, nk=nk_p,
                                  pad_rows=ATT_WINDOW, heads_per_step=8)
            hp = _mix_ffn(hp.reshape(B * T, D), (att.reshape(B * T, D),), (), w_out,
                          rwkv=False, tm=tm_ffn_pa, **ffn_args).reshape(B, T, D)
            k_p.append(k_last.reshape(B, -1, heads, HEAD)); v_p.append(v_last.reshape(B, -1, heads, HEAD))
            ns = Bs * Ts
            q, k16, v16, k_new, v_new = _qkv_proj(hs.reshape(1, ns, D), gn, w_qkv, tq=tm_ffn_s, pad_rows=0, keep=ns)
            att = _sample_attention(q.reshape(Bs, Ts, D), bf(cache_k[j]).reshape(Bs, W, D),
                                    bf(cache_v[j]).reshape(Bs, W, D), k16.reshape(Bs, Ts, D),
                                    v16.reshape(Bs, Ts, D), _bias_lines(attn_rel_bias[j], Ts, W + Ts))
            hs = _mix_ffn(hs.reshape(ns, D), (att.reshape(ns, D),), (), w_out,
                          rwkv=False, tm=tm_ffn_s, **ffn_args).reshape(Bs, Ts, D)
            k_s.append(k_new.reshape(Bs, Ts, heads, HEAD)); v_s.append(v_new.reshape(Bs, Ts, heads, HEAD))

    return (hp, hs,
            jnp.stack(wkv_p), jnp.stack(shift_p), jnp.stack(k_p), jnp.stack(v_p),
            jnp.stack(wkv_s), jnp.stack(shift_s), jnp.stack(k_s), jnp.stack(v_s))
```

```python
import functools

import jax
import jax.numpy as jnp
from jax import lax
from jax.experimental import pallas as pl
from jax.experimental.pallas import tpu as pltpu

F32 = jnp.float32
BF16 = jnp.bfloat16

HEAD = 64
CHUNK = 64
LEFT_CHUNKS = 8
ATT_WINDOW = LEFT_CHUNKS * CHUNK
REL_MAX_PAST = 256
PAST_LEN = 4096
RMS_EPS = 1e-6
LNX_EPS = 64e-5
NEG_INF = -1e30
LOG2E = 1.4426950408889634
WKV_CHUNK = 64
WKV_SUB = 2
SEG_LANES = 128
VMEM_LIMIT = 56 * 1024 * 1024


def _dot(a, b):
    return jnp.dot(a, b, preferred_element_type=F32)


def _dot_nt(a, b):
    return lax.dot_general(a, b, (((1,), (1,)), ((), ())), preferred_element_type=F32)


def _dot_tn(a, b):
    return lax.dot_general(a, b, (((0,), (0,)), ((), ())), preferred_element_type=F32)


def _rms(x, g):
    return x * lax.rsqrt(jnp.mean(x * x, axis=-1, keepdims=True) + RMS_EPS) * g


def _const_spec(shape):
    nd = len(shape)
    return pl.BlockSpec(shape, lambda *_: (0,) * nd, pipeline_mode=pl.Buffered(1))


def _seg_sum(x, seg_ref):
    return _dot(x.astype(BF16), seg_ref[...])


def _seg_expand(s, exp_ref):
    hi = s.astype(BF16)
    lo = (s - hi.astype(F32)).astype(BF16)
    return _dot(jnp.concatenate([hi, lo], axis=-1), exp_ref[...])


def _tmix_proj_kernel(*refs, rows_mode, has_vres, seq_blocks):
    it = iter(refs)
    h_ref = next(it)
    bnd_ref = next(it)
    shift_ref = None if rows_mode else next(it)
    vfirst_ref = next(it) if has_vres else None
    gn_ref = next(it)
    mu_ref = next(it)
    wrkv_ref = next(it)
    w0_ref, w1_ref, w2_ref = next(it), next(it), next(it)
    a0_ref, a1_ref, a2_ref = next(it), next(it), next(it)
    if has_vres:
        v0_ref, v1_ref, v2_ref = next(it), next(it), next(it)
    g1_ref, g2_ref = next(it), next(it)
    kk_ref, ka_ref, rk_ref = next(it), next(it), next(it)
    seg_ref, exp_ref = next(it), next(it)
    (r_out, k_out, v_out, lw_out, kk_out, bb_out, g_out, bonus_out, xl_out) = it

    gn = gn_ref[...]
    x = _rms(h_ref[0], gn)
    tm = x.shape[0]
    row = lax.broadcasted_iota(jnp.int32, x.shape, 0)
    rolled = pltpu.roll(x, 1, 0)
    if rows_mode:
        seq_len = seq_blocks
        x_prev = jnp.where(row % seq_len == 0, bnd_ref[0], rolled)
        xl_out[0] = x
    else:
        halo = bnd_ref[0]
        prev_row = _rms(halo[7:8, :], gn)
        first = pl.program_id(1) % seq_blocks == 0
        prev_row = jnp.where(first, shift_ref[0], prev_row)
        x_prev = jnp.where(row == 0, prev_row, rolled)
        xl_out[0] = x[tm - 1:tm, :]
    xx = x_prev - x

    def mixed(j):
        return (x + xx * mu_ref[j:j + 1, :]).astype(BF16)

    m_v = mixed(2)
    r = _dot(mixed(0), wrkv_ref[0])
    k = _dot(mixed(1), wrkv_ref[1])
    v = _dot(m_v, wrkv_ref[2])

    w_pre = w0_ref[...] + _dot(jnp.tanh(_dot(mixed(3), w1_ref[...])).astype(BF16), w2_ref[...])
    lw = jax.nn.sigmoid(w_pre) * (-0.6065306597126334)
    a = jax.nn.sigmoid(a0_ref[...] + _dot(_dot(mixed(4), a1_ref[...]).astype(BF16), a2_ref[...]))
    g = _dot(jax.nn.sigmoid(_dot(mixed(5), g1_ref[...])).astype(BF16), g2_ref[...])
    if has_vres:
        gate = jax.nn.sigmoid(v0_ref[...] + _dot(_dot(m_v, v1_ref[...]).astype(BF16), v2_ref[...]))
        v = v + (vfirst_ref[0] - v) * gate

    kk = k * kk_ref[...]
    norm = jnp.maximum(jnp.sqrt(_seg_sum(kk * kk, seg_ref)), 1e-12)
    kk = kk * _seg_expand(1.0 / norm, exp_ref)
    k = k * (1.0 + (a - 1.0) * ka_ref[...])
    bonus = _seg_expand(_seg_sum(r * k * rk_ref[...], seg_ref), exp_ref) * v

    r_out[0] = r
    k_out[0] = k
    v_out[0] = v
    lw_out[0] = lw
    kk_out[0] = kk
    bb_out[0] = kk * a
    g_out[0] = g
    bonus_out[0] = bonus


def _tmix_proj(h, shift, vfirst, gn, p, seg, exp, *, tm):
    B, T, D = h.shape
    rows_mode = T < tm
    has_vres = vfirst is not None
    if rows_mode:
        nseq = tm // T
        assert (B * T) % tm == 0
        hb = h.reshape(B * T // tm, tm, D)
        bnd = jnp.zeros((B, T, D), F32).at[:, 0, :].set(shift).reshape(hb.shape)
        grid = (hb.shape[0], 1)
        blk = pl.BlockSpec((1, tm, D), lambda b, i: (b, 0, 0))
        in_arrays = [hb, bnd]
        in_specs = [blk, blk]
        seq_blocks = T
        if has_vres:
            in_arrays.append(vfirst.reshape(hb.shape))
            in_specs.append(blk)
        xl_shape = jax.ShapeDtypeStruct(hb.shape, F32)
        xl_spec = blk
        out_shape_main = hb.shape
    else:
        assert T % tm == 0 and tm % 8 == 0
        nblk = T // tm
        grid = (B, nblk)
        blk = pl.BlockSpec((1, tm, D), lambda b, i: (b, i, 0))
        halo = pl.BlockSpec((1, 8, D), lambda b, i: (b, jnp.maximum(i * (tm // 8) - 1, 0), 0))
        in_arrays = [h, h, shift.reshape(B, 1, D)]
        in_specs = [blk, halo, pl.BlockSpec((1, 1, D), lambda b, i: (b, 0, 0))]
        seq_blocks = nblk
        if has_vres:
            in_arrays.append(vfirst)
            in_specs.append(blk)
        xl_shape = jax.ShapeDtypeStruct((B, 1, D), F32)
        xl_spec = pl.BlockSpec((1, 1, D), lambda b, i: (b, 0, 0))
        out_shape_main = h.shape

    weights = [gn, p["mu"], p["w_rkv"], p["w0"], p["w1"], p["w2"], p["a0"], p["a1"], p["a2"]]
    if has_vres:
        weights += [p["v0"], p["v1"], p["v2"]]
    weights += [p["g1"], p["g2"], p["k_k"], p["k_a"], p["r_k"], seg, exp]
    in_arrays += weights
    in_specs += [_const_spec(w.shape) for w in weights]

    main = jax.ShapeDtypeStruct(out_shape_main, F32)
    outs = pl.pallas_call(
        functools.partial(_tmix_proj_kernel, rows_mode=rows_mode, has_vres=has_vres, seq_blocks=seq_blocks),
        grid=grid,
        in_specs=in_specs,
        out_specs=[blk] * 8 + [xl_spec],
        out_shape=[main] * 8 + [xl_shape],
        compiler_params=pltpu.CompilerParams(
            dimension_semantics=("parallel", "arbitrary"), vmem_limit_bytes=VMEM_LIMIT),
        name="tmix_proj",
    )(*in_arrays)
    main_outs = [o.reshape(B, T, D) for o in outs[:8]]
    if rows_mode:
        x_last = outs[8].reshape(B, T, D)[:, T - 1, :]
    else:
        x_last = outs[8].reshape(B, D)
    return main_outs, x_last


def _wkv_chunk_kernel(r_ref, k_ref, v_ref, lw_ref, kk_ref, bb_ref, s0_ref, o_ref, s_out_ref, s_scr, *,
                      heads, n_stage):
    c = pl.program_id(1)
    C = HEAD
    n_sub = r_ref.shape[1] // C
    W2 = 2 * HEAD

    @pl.when(c == 0)
    def _():
        s_scr[...] = s0_ref[0]

    lane = lax.broadcasted_iota(jnp.int32, (C, W2), 1)
    trow2 = lax.broadcasted_iota(jnp.int32, (C, W2), 0)
    s_idx = lane & (HEAD - 1)
    strict2 = trow2 > s_idx
    incl2 = trow2 >= s_idx
    even_lanes = lane < HEAD
    own = [even_lanes, jnp.logical_not(even_lanes)]
    eye_hi = jnp.where(trow2 == lane - HEAD, 1.0, 0.0)
    zero16 = jnp.zeros((C, W2), BF16)
    pairs = range(heads // 2)
    cols = [slice(p * W2, (p + 1) * W2) for p in pairs]
    hp = [(p, q) for p in pairs for q in range(2)]
    trow = lax.broadcasted_iota(jnp.int32, (C, r_ref.shape[2]), 0)

    zero_s = jnp.zeros((HEAD, W2), BF16)

    def phase_a(ci, sub):
        rows = slice(ci * C, (ci + 1) * C)
        lw = lw_ref[0, rows, :]
        cs = lw
        d = 1
        while d < C:
            cs = cs + jnp.where(trow >= d, pltpu.roll(cs, d, 0), 0.0)
            d *= 2
        p_incl = jnp.exp(cs)
        p_inv = jnp.exp(-cs)
        p_prev = jnp.exp(cs - lw)
        rt = (r_ref[0, rows, :] * p_incl).astype(BF16)
        nt = (kk_ref[0, rows, :] * (-p_prev)).astype(BF16)
        bt = (bb_ref[0, rows, :] * p_inv).astype(BF16)
        kt = (k_ref[0, rows, :] * p_inv).astype(BF16)
        vv = v_ref[0, rows, :].astype(BF16)
        sub.update(p_last=p_incl[C - 1:C, :], bt2=[bt[:, cl] for cl in cols], kt2=[kt[:, cl] for cl in cols],
                   vv2=[vv[:, cl] for cl in cols])
        sub["nr"] = [jnp.concatenate([jnp.where(own[q], nt[:, cols[p]], zero16),
                                      jnp.where(own[q], rt[:, cols[p]], zero16)], axis=0) for p, q in hp]
        g = [_dot_nt(sub["nr"][i], jnp.concatenate([sub["bt2"][p], sub["kt2"][p]], axis=0))
             for i, (p, q) in enumerate(hp)]
        yield
        gt = [jnp.where(strict2, gi[:C], 0.0) for gi in g]
        a_k = [pltpu.roll(gti, HEAD, 1)[:, :HEAD].astype(BF16) for gti in gt]
        sub["av"] = [_dot(a_k[i], sub["vv2"][p]) for i, (p, q) in enumerate(hp)]
        z = [jnp.where(even_lanes, gti, eye_hi) for gti in gt]
        sub["lo"] = [jnp.where(incl2, gi[C:], 0.0).astype(BF16) for gi in g]
        for _ in range(n_stage):
            yield
            zb = [zi.astype(BF16) for zi in z]
            res = [_dot(zbi[:, :HEAD], zbi) for zbi in zb]
            z = [jnp.where(even_lanes, ri, zi + ri) for zi, ri in zip(z, res)]
        sub["t16"] = [pltpu.roll(zi, HEAD, 1)[:, :HEAD].astype(BF16) for zi in z]

    def phase_b(ci, sub, carry):
        rows = slice(ci * C, (ci + 1) * C)
        state = carry["state"]
        s16 = [s.astype(BF16) for s in state]
        x0 = [_dot_nt(sub["nr"][i], jnp.concatenate([s16[p], zero_s] if q == 0 else [zero_s, s16[p]], axis=0))
              for i, (p, q) in enumerate(hp)]
        yield
        w16 = [(x0[i][:C] + sub["av"][i]).astype(BF16) for i in range(len(hp))]
        u16 = [_dot(sub["t16"][i], w16[i]).astype(BF16) for i in range(len(hp))]
        yield
        new_state = []
        for p in pairs:
            lhs = jnp.concatenate([u16[2 * p], u16[2 * p + 1], sub["vv2"][p]], axis=0)
            rhs = jnp.concatenate([jnp.where(own[0], sub["bt2"][p], zero16),
                                   jnp.where(own[1], sub["bt2"][p], zero16), sub["kt2"][p]], axis=0)
            upd = _dot_tn(lhs, rhs)
            upd = jnp.where(even_lanes, upd[:HEAD], upd[HEAD:])
            new_state.append((state[p] + upd) * sub["p_last"][:, cols[p]])
        carry["state"] = new_state
        yield
        outs = [x0[i][C:] + _dot(sub["lo"][i], jnp.concatenate([u16[i], sub["vv2"][p]], axis=0))
                for i, (p, q) in enumerate(hp)]
        for p in pairs:
            o_ref[0, rows, cols[p]] = jnp.where(even_lanes, outs[2 * p], outs[2 * p + 1])

    def run_interleaved(*gens):
        live = list(gens)
        while live:
            for gen in list(live):
                try:
                    next(gen)
                except StopIteration:
                    live.remove(gen)

    carry = dict(state=[s_scr[p] for p in pairs])
    subs = [dict() for _ in range(n_sub)]
    run_interleaved(*[phase_a(ci, subs[ci]) for ci in range(n_sub)])
    for ci in range(n_sub):
        run_interleaved(phase_b(ci, subs[ci], carry))
    for p in pairs:
        s_scr[p] = carry["state"][p]

    @pl.when(c == pl.num_programs(1) - 1)
    def _():
        s_out_ref[0] = s_scr[...]


def _wkv_chunked(r, k, v, lw, kk, bb, s0):
    B, T, D = r.shape
    heads = D // HEAD
    chunk = WKV_CHUNK
    assert chunk == HEAD and heads % 2 == 0
    Tp = -(-T // chunk) * chunk
    if Tp != T:
        r, k, v, lw, kk, bb = (jnp.pad(a, ((0, 0), (0, Tp - T), (0, 0))) for a in (r, k, v, lw, kk, bb))
    pair = lambda s: s.reshape(B, heads // 2, 2, HEAD, HEAD).transpose(0, 1, 3, 2, 4).reshape(
        B, heads // 2, HEAD, 2 * HEAD)
    unpair = lambda s: s.reshape(B, heads // 2, HEAD, 2, HEAD).transpose(0, 1, 3, 2, 4).reshape(
        B, heads, HEAD, HEAD)
    rows = chunk * WKV_SUB if Tp % (chunk * WKV_SUB) == 0 else chunk
    blk = pl.BlockSpec((1, rows, D), lambda b, c: (b, c, 0))
    st = pl.BlockSpec((1, heads // 2, HEAD, 2 * HEAD), lambda b, c: (b, 0, 0, 0))
    o, s_out = pl.pallas_call(
        functools.partial(_wkv_chunk_kernel, heads=heads,
                          n_stage=(min(T, chunk) - 1).bit_length() if Tp == chunk else chunk.bit_length() - 1),
        grid=(B, Tp // rows),
        in_specs=[blk] * 6 + [st],
        out_specs=[blk, st],
        out_shape=[jax.ShapeDtypeStruct((B, Tp, D), F32),
                   jax.ShapeDtypeStruct((B, heads // 2, HEAD, 2 * HEAD), F32)],
        scratch_shapes=[pltpu.VMEM((heads // 2, HEAD, 2 * HEAD), F32)],
        compiler_params=pltpu.CompilerParams(
            dimension_semantics=("parallel", "arbitrary"), vmem_limit_bytes=VMEM_LIMIT),
        name="wkv_chunk",
    )(r, k, v, lw, kk, bb, pair(s0))
    return o[:, :T], unpair(s_out)


def _qkv_kernel(h_ref, gn_ref, w_ref, q_out, kp_out, vp_out, kl_out, vl_out, *, npad, last_from):
    i = pl.program_id(1)
    D = h_ref.shape[2]

    @pl.when(i < npad)
    def _():
        kp_out[0] = jnp.zeros(kp_out.shape[1:], kp_out.dtype)
        vp_out[0] = jnp.zeros(vp_out.shape[1:], vp_out.dtype)

    @pl.when(i >= npad)
    def _():
        xn = _rms(h_ref[0], gn_ref[...]).astype(BF16)
        q = _dot(xn, w_ref[:, 0:D])
        k = _dot(xn, w_ref[:, D:2 * D])
        v = _dot(xn, w_ref[:, 2 * D:3 * D])
        q_out[0] = (q * (HEAD ** -0.5 * LOG2E)).astype(BF16)
        kp_out[0] = k.astype(BF16)
        vp_out[0] = v.astype(BF16)

        @pl.when(i >= npad + last_from)
        def _():
            kl_out[0] = k
            vl_out[0] = v


def _qkv_proj(h, gn, w_qkv, *, tq, pad_rows, keep):
    B, T, D = h.shape
    assert T % tq == 0 and pad_rows % tq == 0
    npad = pad_rows // tq
    assert keep % tq == 0
    last_from = (T - keep) // tq
    blk_in = pl.BlockSpec((1, tq, D), lambda b, i: (b, jnp.maximum(i - npad, 0), 0))
    blk_pad = pl.BlockSpec((1, tq, D), lambda b, i: (b, i, 0))
    blk_last = pl.BlockSpec((1, tq, D), lambda b, i: (b, jnp.maximum(i - npad - last_from, 0), 0))
    return pl.pallas_call(
        functools.partial(_qkv_kernel, npad=npad, last_from=last_from),
        grid=(B, npad + T // tq),
        in_specs=[blk_in, _const_spec(gn.shape), _const_spec(w_qkv.shape)],
        out_specs=[blk_in, blk_pad, blk_pad, blk_last, blk_last],
        out_shape=[jax.ShapeDtypeStruct((B, T, D), BF16),
                   jax.ShapeDtypeStruct((B, pad_rows + T, D), BF16),
                   jax.ShapeDtypeStruct((B, pad_rows + T, D), BF16),
                   jax.ShapeDtypeStruct((B, keep, D), F32),
                   jax.ShapeDtypeStruct((B, keep, D), F32)],
        compiler_params=pltpu.CompilerParams(
            dimension_semantics=("parallel", "arbitrary"), vmem_limit_bytes=VMEM_LIMIT),
        name="qkv_proj",
    )(h, gn, w_qkv)


def _toeplitz(line, rows):
    return pltpu.roll(jnp.broadcast_to(line, (rows, line.shape[1])), 0, 1, stride=1, stride_axis=0)


def _band_attn_kernel(q_ref, k_ref, v_ref, line_ref, o_ref, tbl_ref, *, nk, pad_rows):
    mq = q_ref.shape[1]
    W2 = 2 * HEAD
    n_cols = q_ref.shape[2] // W2
    c = pl.program_id(2)

    @pl.when((pl.program_id(1) == 0) & (c == 0))
    def _():
        qc = lax.broadcasted_iota(jnp.int32, (mq, nk), 0) // CHUNK
        kc = lax.broadcasted_iota(jnp.int32, (mq, nk), 1) // CHUNK
        in_band = (kc >= qc) & (kc <= qc + LEFT_CHUNKS)
        for i in range(tbl_ref.shape[0]):
            tbl_ref[i] = jnp.where(in_band, _toeplitz(line_ref[i], mq)[:, :nk], NEG_INF)

    start = pl.multiple_of(c * mq, mq)
    lane = lax.broadcasted_iota(jnp.int32, (mq, W2), 1)
    even_lanes = lane < HEAD
    own = [even_lanes, jnp.logical_not(even_lanes)]
    cols = [slice(p * W2, (p + 1) * W2) for p in range(n_cols)]
    hp = [(p, q) for p in range(n_cols) for q in range(2)]

    def attend(band_has_padding):
        kb = [k_ref[0, pl.ds(start, nk), cl] for cl in cols]
        vb = [v_ref[0, pl.ds(start, nk), cl] for cl in cols]
        q2 = [q_ref[0, :, cl] for cl in cols]
        s = [_dot_nt(jnp.where(own[q], q2[p], jnp.zeros_like(q2[p])), kb[p]) + tbl_ref[i]
             for i, (p, q) in enumerate(hp)]
        if band_has_padding:
            col = lax.broadcasted_iota(jnp.int32, (1, nk), 1)
            pen = jnp.where(col >= pad_rows - c * mq, 0.0, NEG_INF)
            s = [si + pen for si in s]
        m = [jnp.max(si, axis=-1, keepdims=True) for si in s]
        pr = [jnp.exp2(si - mi) for si, mi in zip(s, m)]
        l = [jnp.sum(pi, axis=-1, keepdims=True) for pi in pr]
        o = [_dot(pr[i].astype(BF16), vb[p]) * (1.0 / l[i]) for i, (p, q) in enumerate(hp)]
        for p in range(n_cols):
            o_ref[0, :, cols[p]] = jnp.where(even_lanes, o[2 * p], o[2 * p + 1]).astype(o_ref.dtype)

    pl.when(c * mq < pad_rows)(lambda: attend(True))
    pl.when(c * mq >= pad_rows)(lambda: attend(False))


def _band_attention(q, kpad, vpad, lines, *, mq, nk, pad_rows, heads_per_step):
    B, T, D = q.shape
    Tk = kpad.shape[1]
    width = heads_per_step * HEAD
    assert T % mq == 0 and D % width == 0 and width % (2 * HEAD) == 0
    q_spec = pl.BlockSpec((1, mq, width), lambda g, b, c: (b, c, g))
    kv_spec = pl.BlockSpec((1, Tk, width), lambda g, b, c: (b, 0, g))
    line_spec = pl.BlockSpec((heads_per_step, 1, lines.shape[2]), lambda g, b, c: (g, 0, 0))
    return pl.pallas_call(
        functools.partial(_band_attn_kernel, nk=nk, pad_rows=pad_rows),
        grid=(D // width, B, T // mq),
        in_specs=[q_spec, kv_spec, kv_spec, line_spec],
        out_specs=q_spec,
        out_shape=jax.ShapeDtypeStruct((B, T, D), BF16),
        scratch_shapes=[pltpu.VMEM((heads_per_step, mq, nk), F32)],
        compiler_params=pltpu.CompilerParams(
            dimension_semantics=("arbitrary", "arbitrary", "arbitrary"), vmem_limit_bytes=VMEM_LIMIT),
        name="band_attn",
    )(q, kpad, vpad, lines)


def _sample_attn_kernel(q_ref, kc_ref, vc_ref, kn_ref, vn_ref, line_ref, o_ref, tc_ref, tn_ref, *, heads):
    W2 = 2 * HEAD
    tq = q_ref.shape[1]
    nc = kc_ref.shape[1]

    @pl.when(pl.program_id(0) == 0)
    def _():
        for i in range(heads):
            t = _toeplitz(line_ref[i], tq)
            tc_ref[i] = t[:, :nc]
            tn_ref[i] = t[:, nc:nc + tq]

    lane = lax.broadcasted_iota(jnp.int32, (tq, W2), 1)
    even_lanes = lane < HEAD
    own = [even_lanes, jnp.logical_not(even_lanes)]
    hp = [(p, q) for p in range(heads // 2) for q in range(2)]
    cols = [slice(p * W2, (p + 1) * W2) for p in range(heads // 2)]
    q2 = [q_ref[0, :, cl] for cl in cols]
    qm = [jnp.where(own[q], q2[p], jnp.zeros_like(q2[p])) for p, q in hp]
    s_c = [_dot_nt(qm[i], kc_ref[0, :, cols[p]]) + tc_ref[i] for i, (p, q) in enumerate(hp)]
    s_n = [_dot_nt(qm[i], kn_ref[0, :, cols[p]]) + tn_ref[i] for i, (p, q) in enumerate(hp)]
    m = [jnp.maximum(jnp.max(a, axis=-1, keepdims=True), jnp.max(b, axis=-1, keepdims=True))
         for a, b in zip(s_c, s_n)]
    p_c = [jnp.exp2(a - mi) for a, mi in zip(s_c, m)]
    p_n = [jnp.exp2(b - mi) for b, mi in zip(s_n, m)]
    l = [jnp.sum(a, axis=-1, keepdims=True) + jnp.sum(b, axis=-1, keepdims=True) for a, b in zip(p_c, p_n)]
    o = [(_dot(p_c[i].astype(BF16), vc_ref[0, :, cols[p]]) + _dot(p_n[i].astype(BF16), vn_ref[0, :, cols[p]]))
         * (1.0 / l[i]) for i, (p, q) in enumerate(hp)]
    for p in range(heads // 2):
        o_ref[0, :, cols[p]] = jnp.where(even_lanes, o[2 * p], o[2 * p + 1]).astype(o_ref.dtype)


def _sample_attention(q, kc, vc, kn, vn, lines):
    B, T, D = q.shape
    W = kc.shape[1]
    heads = D // HEAD
    new = pl.BlockSpec((1, T, D), lambda b: (b, 0, 0))
    old = pl.BlockSpec((1, W, D), lambda b: (b, 0, 0))
    return pl.pallas_call(
        functools.partial(_sample_attn_kernel, heads=heads),
        grid=(B,),
        in_specs=[new, old, old, new, new, _const_spec(lines.shape)],
        out_specs=new,
        out_shape=jax.ShapeDtypeStruct((B, T, D), BF16),
        scratch_shapes=[pltpu.VMEM((heads, T, W), F32), pltpu.VMEM((heads, T, T), F32)],
        compiler_params=pltpu.CompilerParams(dimension_semantics=("arbitrary",), vmem_limit_bytes=VMEM_LIMIT),
        name="sample_attn",
    )(q, kc, vc, kn, vn, lines)


def _bias_lines(rel_bias, mq, nk):
    period = -(-(nk + mq) // 128) * 128
    e = jnp.arange(period, dtype=jnp.int32)
    e = jnp.where(e < nk, e, e - period)
    rel = jnp.clip(ATT_WINDOW - e, -(CHUNK - 1), REL_MAX_PAST) + (CHUNK - 1)
    return (rel_bias.astype(F32)[:, rel] * LOG2E)[:, None, :]


def _mix_ffn_kernel(*refs, rwkv, final, n_ff_chunks):
    it = iter(refs)
    h_ref = next(it)
    if rwkv:
        o_ref, bonus_ref, g_ref = next(it), next(it), next(it)
        lnw_ref, lnb_ref, seg_ref, exp_ref = next(it), next(it), next(it), next(it)
    else:
        a_ref = next(it)
    wo_ref, gf_ref, wgu_ref, wd_ref = next(it), next(it), next(it), next(it)
    gfin_ref = next(it) if final else None
    out_ref = next(it)

    if rwkv:
        o = o_ref[...]
        mean = _seg_expand(_seg_sum(o, seg_ref) * (1.0 / HEAD), exp_ref)
        d = o - mean
        rstd = lax.rsqrt(_seg_sum(d * d, seg_ref) * (1.0 / HEAD) + LNX_EPS)
        y = d * _seg_expand(rstd, exp_ref) * lnw_ref[...] + lnb_ref[...]
        a = ((y + bonus_ref[...]) * g_ref[...]).astype(BF16)
    else:
        a = a_ref[...]
    h1 = h_ref[...] + _dot(a, wo_ref[...])

    xn = _rms(h1, gf_ref[...]).astype(BF16)
    dff = wd_ref.shape[0]
    fc = dff // n_ff_chunks
    acc = h1
    for c in range(n_ff_chunks):
        gate = _dot(xn, wgu_ref[:, c * fc:(c + 1) * fc])
        up = _dot(xn, wgu_ref[:, dff + c * fc:dff + (c + 1) * fc])
        act = (gate * jax.nn.sigmoid(gate) * up).astype(BF16)
        acc = acc + _dot(act, wd_ref[c * fc:(c + 1) * fc, :])
    if final:
        acc = _rms(acc, gfin_ref[...])
    out_ref[...] = acc


def _mix_ffn(h, mix_inputs, mix_weights, w_out, gf, w_gu, w_down, g_final, *, rwkv, tm, n_ff_chunks):
    N, D = h.shape
    assert N % tm == 0
    final = g_final is not None
    row = pl.BlockSpec((tm, D), lambda i: (i, 0))
    consts = list(mix_weights) + [w_out, gf, w_gu, w_down] + ([g_final] if final else [])
    return pl.pallas_call(
        functools.partial(_mix_ffn_kernel, rwkv=rwkv, final=final, n_ff_chunks=n_ff_chunks),
        grid=(N // tm,),
        in_specs=[row] + [row] * len(mix_inputs) + [_const_spec(w.shape) for w in consts],
        out_specs=row,
        out_shape=jax.ShapeDtypeStruct((N, D), F32),
        compiler_params=pltpu.CompilerParams(
            dimension_semantics=("parallel",), vmem_limit_bytes=VMEM_LIMIT),
        name="mix_ffn_rwkv" if rwkv else "mix_ffn_attn",
    )(h, *mix_inputs, *consts)


def _row_block(n, want):
    t = min(want, n)
    while n % t:
        t //= 2
    return t


def kernel(x_prompt, x_sample, state_wkv, state_shift, cache_k, cache_v, norm_mix, norm_ffn, norm_final, rwkv_mu, rwkv_w_rkv, rwkv_w_out, rwkv_decay_w0, rwkv_decay_w1, rwkv_decay_w2, rwkv_iclr_a0, rwkv_iclr_a1, rwkv_iclr_a2, rwkv_vres_v0, rwkv_vres_v1, rwkv_vres_v2, rwkv_gate_g1, rwkv_gate_g2, rwkv_k_k, rwkv_k_a, rwkv_r_k, rwkv_lnx_w, rwkv_lnx_b, attn_w_qkv, attn_w_out, attn_rel_bias, ffn_w_gu, ffn_w_down):
    B, T, D = x_prompt.shape
    Bs, Ts, _ = x_sample.shape
    depth = norm_mix.shape[0]
    heads = D // HEAD
    W = cache_k.shape[2]
    assert W == ATT_WINDOW and Ts <= CHUNK and PAST_LEN % CHUNK == 0

    vec = lambda a: a.reshape(1, D).astype(F32)
    bf = lambda a: a.astype(BF16)
    split_heads = lambda xs, b: jnp.stack(xs).reshape(len(xs), b, -1, heads, HEAD)

    col = jnp.arange(D, dtype=jnp.int32)[:, None] // HEAD
    seg = (col == jnp.arange(SEG_LANES, dtype=jnp.int32)[None, :]).astype(BF16)
    exp = jnp.concatenate([seg.T, seg.T], axis=0)

    hp, hs = x_prompt, x_sample
    vf_p = vf_s = None
    wkv_p, shift_p, k_p, v_p = [], [], [], []
    wkv_s, shift_s, k_s, v_s = [], [], [], []

    tm_proj = _row_block(T, 256)
    tm_ffn_p = _row_block(B * T, 256)
    tm_ffn_pa = _row_block(B * T, 512)
    tm_ffn_s = _row_block(Bs * Ts, 512)
    tq = _row_block(min(T, ATT_WINDOW), 512)
    group = _row_block(T, 4 * CHUNK)
    nk_p = ATT_WINDOW + group

    for layer in range(depth):
        j = layer // 2
        gn = vec(norm_mix[layer])
        last = layer == depth - 1
        ffn_args = dict(gf=vec(norm_ffn[layer]), w_gu=bf(ffn_w_gu[layer]), w_down=bf(ffn_w_down[layer]),
                        g_final=vec(norm_final) if last else None, n_ff_chunks=1)
        if layer % 2 == 0:
            p = dict(mu=jnp.pad(rwkv_mu[j].astype(F32), ((0, 2), (0, 0))), w_rkv=bf(rwkv_w_rkv[j]),
                     w0=vec(rwkv_decay_w0[j]), w1=bf(rwkv_decay_w1[j]), w2=bf(rwkv_decay_w2[j]),
                     a0=vec(rwkv_iclr_a0[j]), a1=bf(rwkv_iclr_a1[j]), a2=bf(rwkv_iclr_a2[j]),
                     g1=bf(rwkv_gate_g1[j]), g2=bf(rwkv_gate_g2[j]),
                     k_k=vec(rwkv_k_k[j]), k_a=vec(rwkv_k_a[j]), r_k=vec(rwkv_r_k[j]))
            if j > 0:
                p.update(v0=vec(rwkv_vres_v0[j - 1]), v1=bf(rwkv_vres_v1[j - 1]), v2=bf(rwkv_vres_v2[j - 1]))
            lnw, lnb, w_out = vec(rwkv_lnx_w[j]), vec(rwkv_lnx_b[j]), bf(rwkv_w_out[j])

            def rwkv_side(h, shift, s0, vfirst, tm_proj, tm_ffn):
                b_, t_, _ = h.shape
                (r, k, v, lw, kk, bb, g, bonus), x_last = _tmix_proj(
                    h, shift, vfirst, gn, p, seg, exp, tm=tm_proj)
                o, s_new = _wkv_chunked(r, k, v, lw, kk, bb, s0)
                rows = lambda a: a.reshape(b_ * t_, D)
                h_new = _mix_ffn(rows(h), (rows(o), rows(bonus), rows(g)), (lnw, lnb, seg, exp), w_out,
                                 rwkv=True, tm=tm_ffn, **ffn_args)
                return h_new.reshape(h.shape), x_last, s_new, (v if vfirst is None else vfirst)

            hp, sh, st, vf_p = rwkv_side(hp, jnp.zeros((B, D), F32), jnp.zeros((B, heads, HEAD, HEAD), F32),
                                         vf_p, tm_proj, tm_ffn_p)
            wkv_p.append(st); shift_p.append(sh)
            hs, sh, st, vf_s = rwkv_side(hs, state_shift[j], state_wkv[j].astype(F32),
                                         vf_s, tm_ffn_s, tm_ffn_s)
            wkv_s.append(st); shift_s.append(sh)
        else:
            w_qkv, w_out = bf(attn_w_qkv[j]), bf(attn_w_out[j])
            q, kpad, vpad, k_last, v_last = _qkv_proj(hp, gn, w_qkv, tq=tq, pad_rows=ATT_WINDOW,
                                                      keep=min(ATT_WINDOW, T))
            att = _band_attention(q, kpad, vpad, _bias_lines(attn_rel_bias[j], group, nk_p), mq=group, nk=nk_p,
                                  pad_rows=ATT_WINDOW, heads_per_step=8)
            hp = _mix_ffn(hp.reshape(B * T, D), (att.reshape(B * T, D),), (), w_out,
                          rwkv=False, tm=tm_ffn_pa, **ffn_args).reshape(B, T, D)
            k_p.append(k_last); v_p.append(v_last)
            ns = Bs * Ts
            q, k16, v16, k_new, v_new = _qkv_proj(hs.reshape(1, ns, D), gn, w_qkv, tq=tm_ffn_s, pad_rows=0, keep=ns)
            att = _sample_attention(q.reshape(Bs, Ts, D), bf(cache_k[j]).reshape(Bs, W, D),
                                    bf(cache_v[j]).reshape(Bs, W, D), k16.reshape(Bs, Ts, D),
                                    v16.reshape(Bs, Ts, D), _bias_lines(attn_rel_bias[j], Ts, W + Ts))
            hs = _mix_ffn(hs.reshape(ns, D), (att.reshape(ns, D),), (), w_out,
                          rwkv=False, tm=tm_ffn_s, **ffn_args).reshape(Bs, Ts, D)
            k_s.append(k_new); v_s.append(v_new)

    return (hp, hs,
            jnp.stack(wkv_p), jnp.stack(shift_p), split_heads(k_p, B), split_heads(v_p, B),
            jnp.stack(wkv_s), jnp.stack(shift_s), split_heads(k_s, Bs), split_heads(v_s, Bs))
```

```python
import functools
import math

import jax
import jax.numpy as jnp
from jax import lax
from jax.experimental import pallas as pl
from jax.experimental.pallas import tpu as pltpu

F32 = jnp.float32
BF16 = jnp.bfloat16

HEAD = 64
CHUNK = 64
LEFT_CHUNKS = 8
ATT_WINDOW = LEFT_CHUNKS * CHUNK
REL_MAX_PAST = 256
PAST_LEN = 4096
RMS_EPS = 1e-6
LNX_EPS = 64e-5
NEG_INF = -1e30
LOG2E = math.log2(math.e)
WKV_CHUNK = 64
WKV_SUB = 2
LANES = 128
SEG_LANES = LANES
VMEM_PHYSICAL = 64 * 1024 * 1024
VMEM_LIMIT = VMEM_PHYSICAL * 7 // 8


def _dot(a, b):
    return jnp.dot(a, b, preferred_element_type=F32)


def _dot_nt(a, b):
    return lax.dot_general(a, b, (((1,), (1,)), ((), ())), preferred_element_type=F32)


def _dot_tn(a, b):
    return lax.dot_general(a, b, (((0,), (0,)), ((), ())), preferred_element_type=F32)


def _rms(x, g):
    return x * lax.rsqrt(jnp.mean(x * x, axis=-1, keepdims=True) + RMS_EPS) * g


def _const_spec(shape):
    nd = len(shape)
    return pl.BlockSpec(shape, lambda *_: (0,) * nd, pipeline_mode=pl.Buffered(1))


def _seg_sum(x, seg_ref):
    return _dot(x.astype(BF16), seg_ref[...])


def _seg_expand(s, exp_ref):
    hi = s.astype(BF16)
    lo = (s - hi.astype(F32)).astype(BF16)
    return _dot(jnp.concatenate([hi, lo], axis=-1), exp_ref[...])


def _tmix_proj_kernel(*refs, rows_mode, has_vres, seq_blocks):
    it = iter(refs)
    h_ref = next(it)
    bnd_ref = next(it)
    shift_ref = None if rows_mode else next(it)
    vfirst_ref = next(it) if has_vres else None
    gn_ref = next(it)
    mu_ref = next(it)
    wrkv_ref = next(it)
    w0_ref, w1_ref, w2_ref = next(it), next(it), next(it)
    a0_ref, a1_ref, a2_ref = next(it), next(it), next(it)
    if has_vres:
        v0_ref, v1_ref, v2_ref = next(it), next(it), next(it)
    g1_ref, g2_ref = next(it), next(it)
    kk_ref, ka_ref, rk_ref = next(it), next(it), next(it)
    seg_ref, exp_ref = next(it), next(it)
    (r_out, k_out, v_out, lw_out, kk_out, bb_out, g_out, bonus_out, xl_out) = it

    gn = gn_ref[...]
    x = _rms(h_ref[0], gn)
    tm = x.shape[0]
    row = lax.broadcasted_iota(jnp.int32, x.shape, 0)
    rolled = pltpu.roll(x, 1, 0)
    if rows_mode:
        seq_len = seq_blocks
        x_prev = jnp.where(row % seq_len == 0, bnd_ref[0], rolled)
        xl_out[0] = x
    else:
        halo = bnd_ref[0]
        prev_row = _rms(halo[7:8, :], gn)
        first = pl.program_id(1) % seq_blocks == 0
        prev_row = jnp.where(first, shift_ref[0], prev_row)
        x_prev = jnp.where(row == 0, prev_row, rolled)
        xl_out[0] = x[tm - 1:tm, :]
    xx = x_prev - x

    def mixed(j):
        return (x + xx * mu_ref[j:j + 1, :]).astype(BF16)

    m_v = mixed(2)
    r = _dot(mixed(0), wrkv_ref[0])
    k = _dot(mixed(1), wrkv_ref[1])
    v = _dot(m_v, wrkv_ref[2])

    w_pre = w0_ref[...] + _dot(jnp.tanh(_dot(mixed(3), w1_ref[...])).astype(BF16), w2_ref[...])
    lw = jax.nn.sigmoid(w_pre) * (-math.exp(-0.5))
    a = jax.nn.sigmoid(a0_ref[...] + _dot(_dot(mixed(4), a1_ref[...]).astype(BF16), a2_ref[...]))
    g = _dot(jax.nn.sigmoid(_dot(mixed(5), g1_ref[...])).astype(BF16), g2_ref[...])
    if has_vres:
        gate = jax.nn.sigmoid(v0_ref[...] + _dot(_dot(m_v, v1_ref[...]).astype(BF16), v2_ref[...]))
        v = v + (vfirst_ref[0] - v) * gate

    kk = k * kk_ref[...]
    norm = jnp.maximum(jnp.sqrt(_seg_sum(kk * kk, seg_ref)), 1e-12)
    kk = kk * _seg_expand(1.0 / norm, exp_ref)
    k = k * (1.0 + (a - 1.0) * ka_ref[...])
    bonus = _seg_expand(_seg_sum(r * k * rk_ref[...], seg_ref), exp_ref) * v

    r_out[0] = r
    k_out[0] = k
    v_out[0] = v
    lw_out[0] = lw
    kk_out[0] = kk
    bb_out[0] = kk * a
    g_out[0] = g
    bonus_out[0] = bonus


def _tmix_proj(h, shift, vfirst, gn, p, seg, exp, *, tm):
    B, T, D = h.shape
    rows_mode = T < tm
    has_vres = vfirst is not None
    if rows_mode:
        nseq = tm // T
        assert (B * T) % tm == 0
        hb = h.reshape(B * T // tm, tm, D)
        bnd = jnp.zeros((B, T, D), F32).at[:, 0, :].set(shift).reshape(hb.shape)
        grid = (hb.shape[0], 1)
        blk = pl.BlockSpec((1, tm, D), lambda b, i: (b, 0, 0))
        in_arrays = [hb, bnd]
        in_specs = [blk, blk]
        seq_blocks = T
        if has_vres:
            in_arrays.append(vfirst.reshape(hb.shape))
            in_specs.append(blk)
        xl_shape = jax.ShapeDtypeStruct(hb.shape, F32)
        xl_spec = blk
        out_shape_main = hb.shape
    else:
        assert T % tm == 0 and tm % 8 == 0
        nblk = T // tm
        grid = (B, nblk)
        blk = pl.BlockSpec((1, tm, D), lambda b, i: (b, i, 0))
        halo = pl.BlockSpec((1, 8, D), lambda b, i: (b, jnp.maximum(i * (tm // 8) - 1, 0), 0))
        in_arrays = [h, h, shift.reshape(B, 1, D)]
        in_specs = [blk, halo, pl.BlockSpec((1, 1, D), lambda b, i: (b, 0, 0))]
        seq_blocks = nblk
        if has_vres:
            in_arrays.append(vfirst)
            in_specs.append(blk)
        xl_shape = jax.ShapeDtypeStruct((B, 1, D), F32)
        xl_spec = pl.BlockSpec((1, 1, D), lambda b, i: (b, 0, 0))
        out_shape_main = h.shape

    weights = [gn, p["mu"], p["w_rkv"], p["w0"], p["w1"], p["w2"], p["a0"], p["a1"], p["a2"]]
    if has_vres:
        weights += [p["v0"], p["v1"], p["v2"]]
    weights += [p["g1"], p["g2"], p["k_k"], p["k_a"], p["r_k"], seg, exp]
    in_arrays += weights
    in_specs += [_const_spec(w.shape) for w in weights]

    main = jax.ShapeDtypeStruct(out_shape_main, F32)
    outs = pl.pallas_call(
        functools.partial(_tmix_proj_kernel, rows_mode=rows_mode, has_vres=has_vres, seq_blocks=seq_blocks),
        grid=grid,
        in_specs=in_specs,
        out_specs=[blk] * 8 + [xl_spec],
        out_shape=[main] * 8 + [xl_shape],
        compiler_params=pltpu.CompilerParams(
            dimension_semantics=("parallel", "arbitrary"), vmem_limit_bytes=VMEM_LIMIT),
        name="tmix_proj",
    )(*in_arrays)
    main_outs = [o.reshape(B, T, D) for o in outs[:8]]
    if rows_mode:
        x_last = outs[8].reshape(B, T, D)[:, T - 1, :]
    else:
        x_last = outs[8].reshape(B, D)
    return main_outs, x_last


def _wkv_chunk_kernel(r_ref, k_ref, v_ref, lw_ref, kk_ref, bb_ref, s0_ref, o_ref, s_out_ref, s_scr, *,
                      heads, n_stage):
    c = pl.program_id(1)
    C = HEAD
    n_sub = r_ref.shape[1] // C
    W2 = 2 * HEAD

    @pl.when(c == 0)
    def _():
        s_scr[...] = s0_ref[0]

    lane = lax.broadcasted_iota(jnp.int32, (C, W2), 1)
    trow2 = lax.broadcasted_iota(jnp.int32, (C, W2), 0)
    s_idx = lane & (HEAD - 1)
    strict2 = trow2 > s_idx
    incl2 = trow2 >= s_idx
    even_lanes = lane < HEAD
    own = [even_lanes, jnp.logical_not(even_lanes)]
    eye_hi = jnp.where(trow2 == lane - HEAD, 1.0, 0.0)
    zero16 = jnp.zeros((C, W2), BF16)
    pairs = range(heads // 2)
    cols = [slice(p * W2, (p + 1) * W2) for p in pairs]
    hp = [(p, q) for p in pairs for q in range(2)]
    trow = lax.broadcasted_iota(jnp.int32, (C, r_ref.shape[2]), 0)

    zero_s = jnp.zeros((HEAD, W2), BF16)

    def phase_a(ci, sub):
        rows = slice(ci * C, (ci + 1) * C)
        lw = lw_ref[0, rows, :]
        cs = lw
        d = 1
        while d < C:
            cs = cs + jnp.where(trow >= d, pltpu.roll(cs, d, 0), 0.0)
            d *= 2
        p_incl = jnp.exp(cs)
        p_inv = jnp.exp(-cs)
        p_prev = jnp.exp(cs - lw)
        rt = (r_ref[0, rows, :] * p_incl).astype(BF16)
        nt = (kk_ref[0, rows, :] * (-p_prev)).astype(BF16)
        bt = (bb_ref[0, rows, :] * p_inv).astype(BF16)
        kt = (k_ref[0, rows, :] * p_inv).astype(BF16)
        vv = v_ref[0, rows, :].astype(BF16)
        sub.update(p_last=p_incl[C - 1:C, :], bt2=[bt[:, cl] for cl in cols], kt2=[kt[:, cl] for cl in cols],
                   vv2=[vv[:, cl] for cl in cols])
        sub["nr"] = [jnp.concatenate([jnp.where(own[q], nt[:, cols[p]], zero16),
                                      jnp.where(own[q], rt[:, cols[p]], zero16)], axis=0) for p, q in hp]
        g = [_dot_nt(sub["nr"][i], jnp.concatenate([sub["bt2"][p], sub["kt2"][p]], axis=0))
             for i, (p, q) in enumerate(hp)]
        yield
        gt = [jnp.where(strict2, gi[:C], 0.0) for gi in g]
        a_k = [pltpu.roll(gti, HEAD, 1)[:, :HEAD].astype(BF16) for gti in gt]
        sub["av"] = [_dot(a_k[i], sub["vv2"][p]) for i, (p, q) in enumerate(hp)]
        z = [jnp.where(even_lanes, gti, eye_hi) for gti in gt]
        sub["lo"] = [jnp.where(incl2, gi[C:], 0.0).astype(BF16) for gi in g]
        for _ in range(n_stage):
            yield
            zb = [zi.astype(BF16) for zi in z]
            res = [_dot(zbi[:, :HEAD], zbi) for zbi in zb]
            z = [jnp.where(even_lanes, ri, zi + ri) for zi, ri in zip(z, res)]
        sub["t16"] = [pltpu.roll(zi, HEAD, 1)[:, :HEAD].astype(BF16) for zi in z]

    def phase_b(ci, sub, carry):
        rows = slice(ci * C, (ci + 1) * C)
        state = carry["state"]
        s16 = [s.astype(BF16) for s in state]
        x0 = [_dot_nt(sub["nr"][i], jnp.concatenate([s16[p], zero_s] if q == 0 else [zero_s, s16[p]], axis=0))
              for i, (p, q) in enumerate(hp)]
        yield
        w16 = [(x0[i][:C] + sub["av"][i]).astype(BF16) for i in range(len(hp))]
        u16 = [_dot(sub["t16"][i], w16[i]).astype(BF16) for i in range(len(hp))]
        yield
        new_state = []
        for p in pairs:
            lhs = jnp.concatenate([u16[2 * p], u16[2 * p + 1], sub["vv2"][p]], axis=0)
            rhs = jnp.concatenate([jnp.where(own[0], sub["bt2"][p], zero16),
                                   jnp.where(own[1], sub["bt2"][p], zero16), sub["kt2"][p]], axis=0)
            upd = _dot_tn(lhs, rhs)
            upd = jnp.where(even_lanes, upd[:HEAD], upd[HEAD:])
            new_state.append((state[p] + upd) * sub["p_last"][:, cols[p]])
        carry["state"] = new_state
        yield
        outs = [x0[i][C:] + _dot(sub["lo"][i], jnp.concatenate([u16[i], sub["vv2"][p]], axis=0))
                for i, (p, q) in enumerate(hp)]
        for p in pairs:
            o_ref[0, rows, cols[p]] = jnp.where(even_lanes, outs[2 * p], outs[2 * p + 1])

    def run_interleaved(*gens):
        live = list(gens)
        while live:
            for gen in list(live):
                try:
                    next(gen)
                except StopIteration:
                    live.remove(gen)

    carry = dict(state=[s_scr[p] for p in pairs])
    subs = [dict() for _ in range(n_sub)]
    run_interleaved(*[phase_a(ci, subs[ci]) for ci in range(n_sub)])
    for ci in range(n_sub):
        run_interleaved(phase_b(ci, subs[ci], carry))
    for p in pairs:
        s_scr[p] = carry["state"][p]

    @pl.when(c == pl.num_programs(1) - 1)
    def _():
        s_out_ref[0] = s_scr[...]


def _wkv_chunked(r, k, v, lw, kk, bb, s0):
    B, T, D = r.shape
    heads = D // HEAD
    chunk = WKV_CHUNK
    assert chunk == HEAD and heads % 2 == 0
    Tp = -(-T // chunk) * chunk
    if Tp != T:
        r, k, v, lw, kk, bb = (jnp.pad(a, ((0, 0), (0, Tp - T), (0, 0))) for a in (r, k, v, lw, kk, bb))
    pair = lambda s: s.reshape(B, heads // 2, 2, HEAD, HEAD).transpose(0, 1, 3, 2, 4).reshape(
        B, heads // 2, HEAD, 2 * HEAD)
    unpair = lambda s: s.reshape(B, heads // 2, HEAD, 2, HEAD).transpose(0, 1, 3, 2, 4).reshape(
        B, heads, HEAD, HEAD)
    rows = chunk * WKV_SUB if Tp % (chunk * WKV_SUB) == 0 else chunk
    blk = pl.BlockSpec((1, rows, D), lambda b, c: (b, c, 0))
    st = pl.BlockSpec((1, heads // 2, HEAD, 2 * HEAD), lambda b, c: (b, 0, 0, 0))
    o, s_out = pl.pallas_call(
        functools.partial(_wkv_chunk_kernel, heads=heads,
                          n_stage=(min(T, chunk) - 1).bit_length() if Tp == chunk else chunk.bit_length() - 1),
        grid=(B, Tp // rows),
        in_specs=[blk] * 6 + [st],
        out_specs=[blk, st],
        out_shape=[jax.ShapeDtypeStruct((B, Tp, D), F32),
                   jax.ShapeDtypeStruct((B, heads // 2, HEAD, 2 * HEAD), F32)],
        scratch_shapes=[pltpu.VMEM((heads // 2, HEAD, 2 * HEAD), F32)],
        compiler_params=pltpu.CompilerParams(
            dimension_semantics=("parallel", "arbitrary"), vmem_limit_bytes=VMEM_LIMIT),
        name="wkv_chunk",
    )(r, k, v, lw, kk, bb, pair(s0))
    return o[:, :T], unpair(s_out)


def _qkv_kernel(h_ref, gn_ref, w_ref, q_out, kp_out, vp_out, kl_out, vl_out, *, npad, last_from):
    i = pl.program_id(1)
    D = h_ref.shape[2]

    @pl.when(i < npad)
    def _():
        kp_out[0] = jnp.zeros(kp_out.shape[1:], kp_out.dtype)
        vp_out[0] = jnp.zeros(vp_out.shape[1:], vp_out.dtype)

    @pl.when(i >= npad)
    def _():
        xn = _rms(h_ref[0], gn_ref[...]).astype(BF16)
        q = _dot(xn, w_ref[:, 0:D])
        k = _dot(xn, w_ref[:, D:2 * D])
        v = _dot(xn, w_ref[:, 2 * D:3 * D])
        q_out[0] = (q * (HEAD ** -0.5 * LOG2E)).astype(BF16)
        kp_out[0] = k.astype(BF16)
        vp_out[0] = v.astype(BF16)

        @pl.when(i >= npad + last_from)
        def _():
            kl_out[0] = k
            vl_out[0] = v


def _qkv_proj(h, gn, w_qkv, *, tq, pad_rows, keep):
    B, T, D = h.shape
    assert T % tq == 0 and pad_rows % tq == 0
    npad = pad_rows // tq
    assert keep % tq == 0
    last_from = (T - keep) // tq
    blk_in = pl.BlockSpec((1, tq, D), lambda b, i: (b, jnp.maximum(i - npad, 0), 0))
    blk_pad = pl.BlockSpec((1, tq, D), lambda b, i: (b, i, 0))
    blk_last = pl.BlockSpec((1, tq, D), lambda b, i: (b, jnp.maximum(i - npad - last_from, 0), 0))
    return pl.pallas_call(
        functools.partial(_qkv_kernel, npad=npad, last_from=last_from),
        grid=(B, npad + T // tq),
        in_specs=[blk_in, _const_spec(gn.shape), _const_spec(w_qkv.shape)],
        out_specs=[blk_in, blk_pad, blk_pad, blk_last, blk_last],
        out_shape=[jax.ShapeDtypeStruct((B, T, D), BF16),
                   jax.ShapeDtypeStruct((B, pad_rows + T, D), BF16),
                   jax.ShapeDtypeStruct((B, pad_rows + T, D), BF16),
                   jax.ShapeDtypeStruct((B, keep, D), F32),
                   jax.ShapeDtypeStruct((B, keep, D), F32)],
        compiler_params=pltpu.CompilerParams(
            dimension_semantics=("parallel", "arbitrary"), vmem_limit_bytes=VMEM_LIMIT),
        name="qkv_proj",
    )(h, gn, w_qkv)


def _toeplitz(line, rows):
    return pltpu.roll(jnp.broadcast_to(line, (rows, line.shape[1])), 0, 1, stride=1, stride_axis=0)


def _band_attn_kernel(q_ref, k_ref, v_ref, line_ref, o_ref, tbl_ref, *, nk, pad_rows):
    mq = q_ref.shape[1]
    W2 = 2 * HEAD
    n_cols = q_ref.shape[2] // W2
    c = pl.program_id(2)

    @pl.when((pl.program_id(1) == 0) & (c == 0))
    def _():
        qc = lax.broadcasted_iota(jnp.int32, (mq, nk), 0) // CHUNK
        kc = lax.broadcasted_iota(jnp.int32, (mq, nk), 1) // CHUNK
        in_band = (kc >= qc) & (kc <= qc + LEFT_CHUNKS)
        for i in range(tbl_ref.shape[0]):
            tbl_ref[i] = jnp.where(in_band, _toeplitz(line_ref[i], mq)[:, :nk], NEG_INF)

    start = pl.multiple_of(c * mq, mq)
    lane = lax.broadcasted_iota(jnp.int32, (mq, W2), 1)
    even_lanes = lane < HEAD
    own = [even_lanes, jnp.logical_not(even_lanes)]
    cols = [slice(p * W2, (p + 1) * W2) for p in range(n_cols)]
    hp = [(p, q) for p in range(n_cols) for q in range(2)]

    def attend(band_has_padding):
        kb = [k_ref[0, pl.ds(start, nk), cl] for cl in cols]
        vb = [v_ref[0, pl.ds(start, nk), cl] for cl in cols]
        q2 = [q_ref[0, :, cl] for cl in cols]
        s = [_dot_nt(jnp.where(own[q], q2[p], jnp.zeros_like(q2[p])), kb[p]) + tbl_ref[i]
             for i, (p, q) in enumerate(hp)]
        if band_has_padding:
            col = lax.broadcasted_iota(jnp.int32, (1, nk), 1)
            pen = jnp.where(col >= pad_rows - c * mq, 0.0, NEG_INF)
            s = [si + pen for si in s]
        m = [jnp.max(si, axis=-1, keepdims=True) for si in s]
        pr = [jnp.exp2(si - mi) for si, mi in zip(s, m)]
        l = [jnp.sum(pi, axis=-1, keepdims=True) for pi in pr]
        o = [_dot(pr[i].astype(BF16), vb[p]) * (1.0 / l[i]) for i, (p, q) in enumerate(hp)]
        for p in range(n_cols):
            o_ref[0, :, cols[p]] = jnp.where(even_lanes, o[2 * p], o[2 * p + 1]).astype(o_ref.dtype)

    pl.when(c * mq < pad_rows)(lambda: attend(True))
    pl.when(c * mq >= pad_rows)(lambda: attend(False))


def _band_attention(q, kpad, vpad, lines, *, mq, nk, pad_rows, heads_per_step):
    B, T, D = q.shape
    Tk = kpad.shape[1]
    width = heads_per_step * HEAD
    assert T % mq == 0 and D % width == 0 and width % (2 * HEAD) == 0
    q_spec = pl.BlockSpec((1, mq, width), lambda g, b, c: (b, c, g))
    kv_spec = pl.BlockSpec((1, Tk, width), lambda g, b, c: (b, 0, g))
    line_spec = pl.BlockSpec((heads_per_step, 1, lines.shape[2]), lambda g, b, c: (g, 0, 0))
    return pl.pallas_call(
        functools.partial(_band_attn_kernel, nk=nk, pad_rows=pad_rows),
        grid=(D // width, B, T // mq),
        in_specs=[q_spec, kv_spec, kv_spec, line_spec],
        out_specs=q_spec,
        out_shape=jax.ShapeDtypeStruct((B, T, D), BF16),
        scratch_shapes=[pltpu.VMEM((heads_per_step, mq, nk), F32)],
        compiler_params=pltpu.CompilerParams(
            dimension_semantics=("arbitrary", "arbitrary", "arbitrary"), vmem_limit_bytes=VMEM_LIMIT),
        name="band_attn",
    )(q, kpad, vpad, lines)


def _sample_attn_kernel(q_ref, kc_ref, vc_ref, kn_ref, vn_ref, line_ref, o_ref, tc_ref, tn_ref, *, heads):
    W2 = 2 * HEAD
    tq = q_ref.shape[1]
    nc = kc_ref.shape[1]

    @pl.when(pl.program_id(0) == 0)
    def _():
        for i in range(heads):
            t = _toeplitz(line_ref[i], tq)
            tc_ref[i] = t[:, :nc]
            tn_ref[i] = t[:, nc:nc + tq]

    lane = lax.broadcasted_iota(jnp.int32, (tq, W2), 1)
    even_lanes = lane < HEAD
    own = [even_lanes, jnp.logical_not(even_lanes)]
    hp = [(p, q) for p in range(heads // 2) for q in range(2)]
    cols = [slice(p * W2, (p + 1) * W2) for p in range(heads // 2)]
    q2 = [q_ref[0, :, cl] for cl in cols]
    qm = [jnp.where(own[q], q2[p], jnp.zeros_like(q2[p])) for p, q in hp]
    s_c = [_dot_nt(qm[i], kc_ref[0, :, cols[p]]) + tc_ref[i] for i, (p, q) in enumerate(hp)]
    s_n = [_dot_nt(qm[i], kn_ref[0, :, cols[p]]) + tn_ref[i] for i, (p, q) in enumerate(hp)]
    m = [jnp.maximum(jnp.max(a, axis=-1, keepdims=True), jnp.max(b, axis=-1, keepdims=True))
         for a, b in zip(s_c, s_n)]
    p_c = [jnp.exp2(a - mi) for a, mi in zip(s_c, m)]
    p_n = [jnp.exp2(b - mi) for b, mi in zip(s_n, m)]
    l = [jnp.sum(a, axis=-1, keepdims=True) + jnp.sum(b, axis=-1, keepdims=True) for a, b in zip(p_c, p_n)]
    o = [(_dot(p_c[i].astype(BF16), vc_ref[0, :, cols[p]]) + _dot(p_n[i].astype(BF16), vn_ref[0, :, cols[p]]))
         * (1.0 / l[i]) for i, (p, q) in enumerate(hp)]
    for p in range(heads // 2):
        o_ref[0, :, cols[p]] = jnp.where(even_lanes, o[2 * p], o[2 * p + 1]).astype(o_ref.dtype)


def _sample_attention(q, kc, vc, kn, vn, lines):
    B, T, D = q.shape
    W = kc.shape[1]
    heads = D // HEAD
    new = pl.BlockSpec((1, T, D), lambda b: (b, 0, 0))
    old = pl.BlockSpec((1, W, D), lambda b: (b, 0, 0))
    return pl.pallas_call(
        functools.partial(_sample_attn_kernel, heads=heads),
        grid=(B,),
        in_specs=[new, old, old, new, new, _const_spec(lines.shape)],
        out_specs=new,
        out_shape=jax.ShapeDtypeStruct((B, T, D), BF16),
        scratch_shapes=[pltpu.VMEM((heads, T, W), F32), pltpu.VMEM((heads, T, T), F32)],
        compiler_params=pltpu.CompilerParams(dimension_semantics=("arbitrary",), vmem_limit_bytes=VMEM_LIMIT),
        name="sample_attn",
    )(q, kc, vc, kn, vn, lines)


def _bias_lines(rel_bias, mq, nk):
    period = -(-(nk + mq) // LANES) * LANES
    e = jnp.arange(period, dtype=jnp.int32)
    e = jnp.where(e < nk, e, e - period)
    rel = jnp.clip(ATT_WINDOW - e, -(CHUNK - 1), REL_MAX_PAST) + (CHUNK - 1)
    return (rel_bias.astype(F32)[:, rel] * LOG2E)[:, None, :]


def _mix_ffn_kernel(*refs, rwkv, final, n_ff_chunks):
    it = iter(refs)
    h_ref = next(it)
    if rwkv:
        o_ref, bonus_ref, g_ref = next(it), next(it), next(it)
        lnw_ref, lnb_ref, seg_ref, exp_ref = next(it), next(it), next(it), next(it)
    else:
        a_ref = next(it)
    wo_ref, gf_ref, wgu_ref, wd_ref = next(it), next(it), next(it), next(it)
    gfin_ref = next(it) if final else None
    out_ref = next(it)

    if rwkv:
        o = o_ref[...]
        mean = _seg_expand(_seg_sum(o, seg_ref) * (1.0 / HEAD), exp_ref)
        d = o - mean
        rstd = lax.rsqrt(_seg_sum(d * d, seg_ref) * (1.0 / HEAD) + LNX_EPS)
        y = d * _seg_expand(rstd, exp_ref) * lnw_ref[...] + lnb_ref[...]
        a = ((y + bonus_ref[...]) * g_ref[...]).astype(BF16)
    else:
        a = a_ref[...]
    h1 = h_ref[...] + _dot(a, wo_ref[...])

    xn = _rms(h1, gf_ref[...]).astype(BF16)
    dff = wd_ref.shape[0]
    fc = dff // n_ff_chunks
    acc = h1
    for c in range(n_ff_chunks):
        gate = _dot(xn, wgu_ref[:, c * fc:(c + 1) * fc])
        up = _dot(xn, wgu_ref[:, dff + c * fc:dff + (c + 1) * fc])
        act = (gate * jax.nn.sigmoid(gate) * up).astype(BF16)
        acc = acc + _dot(act, wd_ref[c * fc:(c + 1) * fc, :])
    if final:
        acc = _rms(acc, gfin_ref[...])
    out_ref[...] = acc


def _mix_ffn(h, mix_inputs, mix_weights, w_out, gf, w_gu, w_down, g_final, *, rwkv, tm, n_ff_chunks):
    N, D = h.shape
    assert N % tm == 0
    final = g_final is not None
    row = pl.BlockSpec((tm, D), lambda i: (i, 0))
    consts = list(mix_weights) + [w_out, gf, w_gu, w_down] + ([g_final] if final else [])
    return pl.pallas_call(
        functools.partial(_mix_ffn_kernel, rwkv=rwkv, final=final, n_ff_chunks=n_ff_chunks),
        grid=(N // tm,),
        in_specs=[row] + [row] * len(mix_inputs) + [_const_spec(w.shape) for w in consts],
        out_specs=row,
        out_shape=jax.ShapeDtypeStruct((N, D), F32),
        compiler_params=pltpu.CompilerParams(
            dimension_semantics=("parallel",), vmem_limit_bytes=VMEM_LIMIT),
        name="mix_ffn_rwkv" if rwkv else "mix_ffn_attn",
    )(h, *mix_inputs, *consts)


def _row_block(n, want):
    t = min(want, n)
    while n % t:
        t //= 2
    return t


def kernel(x_prompt, x_sample, state_wkv, state_shift, cache_k, cache_v, norm_mix, norm_ffn, norm_final, rwkv_mu, rwkv_w_rkv, rwkv_w_out, rwkv_decay_w0, rwkv_decay_w1, rwkv_decay_w2, rwkv_iclr_a0, rwkv_iclr_a1, rwkv_iclr_a2, rwkv_vres_v0, rwkv_vres_v1, rwkv_vres_v2, rwkv_gate_g1, rwkv_gate_g2, rwkv_k_k, rwkv_k_a, rwkv_r_k, rwkv_lnx_w, rwkv_lnx_b, attn_w_qkv, attn_w_out, attn_rel_bias, ffn_w_gu, ffn_w_down):
    B, T, D = x_prompt.shape
    Bs, Ts, _ = x_sample.shape
    depth = norm_mix.shape[0]
    heads = D // HEAD
    W = cache_k.shape[2]
    assert W == ATT_WINDOW and Ts <= CHUNK and PAST_LEN % CHUNK == 0

    vec = lambda a: a.reshape(1, D).astype(F32)
    bf = lambda a: a.astype(BF16)
    split_heads = lambda xs, b: jnp.stack(xs).reshape(len(xs), b, -1, heads, HEAD)

    col = jnp.arange(D, dtype=jnp.int32)[:, None] // HEAD
    seg = (col == jnp.arange(SEG_LANES, dtype=jnp.int32)[None, :]).astype(BF16)
    exp = jnp.concatenate([seg.T, seg.T], axis=0)

    hp, hs = x_prompt, x_sample
    vf_p = vf_s = None
    wkv_p, shift_p, k_p, v_p = [], [], [], []
    wkv_s, shift_s, k_s, v_s = [], [], [], []

    tm_proj = _row_block(T, 256)
    tm_ffn_p = _row_block(B * T, 256)
    tm_ffn_pa = _row_block(B * T, 512)
    tm_ffn_s = _row_block(Bs * Ts, 512)
    tq = _row_block(min(T, ATT_WINDOW), 512)
    group = _row_block(T, 4 * CHUNK)
    nk_p = ATT_WINDOW + group

    for layer in range(depth):
        j = layer // 2
        gn = vec(norm_mix[layer])
        last = layer == depth - 1
        ffn_args = dict(gf=vec(norm_ffn[layer]), w_gu=bf(ffn_w_gu[layer]), w_down=bf(ffn_w_down[layer]),
                        g_final=vec(norm_final) if last else None, n_ff_chunks=1)
        if layer % 2 == 0:
            p = dict(mu=jnp.pad(rwkv_mu[j].astype(F32), ((0, 2), (0, 0))), w_rkv=bf(rwkv_w_rkv[j]),
                     w0=vec(rwkv_decay_w0[j]), w1=bf(rwkv_decay_w1[j]), w2=bf(rwkv_decay_w2[j]),
                     a0=vec(rwkv_iclr_a0[j]), a1=bf(rwkv_iclr_a1[j]), a2=bf(rwkv_iclr_a2[j]),
                     g1=bf(rwkv_gate_g1[j]), g2=bf(rwkv_gate_g2[j]),
                     k_k=vec(rwkv_k_k[j]), k_a=vec(rwkv_k_a[j]), r_k=vec(rwkv_r_k[j]))
            if j > 0:
                p.update(v0=vec(rwkv_vres_v0[j - 1]), v1=bf(rwkv_vres_v1[j - 1]), v2=bf(rwkv_vres_v2[j - 1]))
            lnw, lnb, w_out = vec(rwkv_lnx_w[j]), vec(rwkv_lnx_b[j]), bf(rwkv_w_out[j])

            def rwkv_side(h, shift, s0, vfirst, tm_proj, tm_ffn):
                b_, t_, _ = h.shape
                (r, k, v, lw, kk, bb, g, bonus), x_last = _tmix_proj(
                    h, shift, vfirst, gn, p, seg, exp, tm=tm_proj)
                o, s_new = _wkv_chunked(r, k, v, lw, kk, bb, s0)
                rows = lambda a: a.reshape(b_ * t_, D)
                h_new = _mix_ffn(rows(h), (rows(o), rows(bonus), rows(g)), (lnw, lnb, seg, exp), w_out,
                                 rwkv=True, tm=tm_ffn, **ffn_args)
                return h_new.reshape(h.shape), x_last, s_new, (v if vfirst is None else vfirst)

            hp, sh, st, vf_p = rwkv_side(hp, jnp.zeros((B, D), F32), jnp.zeros((B, heads, HEAD, HEAD), F32),
                                         vf_p, tm_proj, tm_ffn_p)
            wkv_p.append(st); shift_p.append(sh)
            hs, sh, st, vf_s = rwkv_side(hs, state_shift[j], state_wkv[j].astype(F32),
                                         vf_s, tm_ffn_s, tm_ffn_s)
            wkv_s.append(st); shift_s.append(sh)
        else:
            w_qkv, w_out = bf(attn_w_qkv[j]), bf(attn_w_out[j])
            q, kpad, vpad, k_last, v_last = _qkv_proj(hp, gn, w_qkv, tq=tq, pad_rows=ATT_WINDOW,
                                                      keep=min(ATT_WINDOW, T))
            att = _band_attention(q, kpad, vpad, _bias_lines(attn_rel_bias[j], group, nk_p), mq=group, nk=nk_p,
                                  pad_rows=ATT_WINDOW, heads_per_step=16)
            hp = _mix_ffn(hp.reshape(B * T, D), (att.reshape(B * T, D),), (), w_out,
                          rwkv=False, tm=tm_ffn_pa, **ffn_args).reshape(B, T, D)
            k_p.append(k_last); v_p.append(v_last)
            ns = Bs * Ts
            q, k16, v16, k_new, v_new = _qkv_proj(hs.reshape(1, ns, D), gn, w_qkv, tq=tm_ffn_s, pad_rows=0, keep=ns)
            att = _sample_attention(q.reshape(Bs, Ts, D), bf(cache_k[j]).reshape(Bs, W, D),
                                    bf(cache_v[j]).reshape(Bs, W, D), k16.reshape(Bs, Ts, D),
                                    v16.reshape(Bs, Ts, D), _bias_lines(attn_rel_bias[j], Ts, W + Ts))
            hs = _mix_ffn(hs.reshape(ns, D), (att.reshape(ns, D),), (), w_out,
                          rwkv=False, tm=tm_ffn_s, **ffn_args).reshape(Bs, Ts, D)
            k_s.append(k_new); v_s.append(v_new)

    return (hp, hs,
            jnp.stack(wkv_p), jnp.stack(shift_p), split_heads(k_p, B), split_heads(v_p, B),
            jnp.stack(wkv_s), jnp.stack(shift_s), split_heads(k_s, Bs), split_heads(v_s, Bs))
```

```python
import functools
import math

import jax
import jax.numpy as jnp
from jax import lax
from jax.experimental import pallas as pl
from jax.experimental.pallas import tpu as pltpu

F32 = jnp.float32
BF16 = jnp.bfloat16

HEAD = 64
CHUNK = 64
LEFT_CHUNKS = 8
ATT_WINDOW = LEFT_CHUNKS * CHUNK
REL_MAX_PAST = 256
PAST_LEN = 4096
RMS_EPS = 1e-6
LNX_EPS = 64e-5
NEG_INF = -1e30
LOG2E = math.log2(math.e)
WKV_CHUNK = 64
WKV_SUB = 2
LANES = 128
SEG_LANES = LANES
VMEM_PHYSICAL = 64 * 1024 * 1024
VMEM_LIMIT = VMEM_PHYSICAL * 7 // 8


def _dot(a, b):
    return jnp.dot(a, b, preferred_element_type=F32)


def _dot_nt(a, b):
    return lax.dot_general(a, b, (((1,), (1,)), ((), ())), preferred_element_type=F32)


def _dot_tn(a, b):
    return lax.dot_general(a, b, (((0,), (0,)), ((), ())), preferred_element_type=F32)


def _rms(x, g):
    return x * lax.rsqrt(jnp.mean(x * x, axis=-1, keepdims=True) + RMS_EPS) * g


def _const_spec(shape):
    nd = len(shape)
    return pl.BlockSpec(shape, lambda *_: (0,) * nd, pipeline_mode=pl.Buffered(1))


def _seg_sum(x, seg_ref):
    return _dot(x.astype(BF16), seg_ref[...])


def _seg_expand(s, exp_ref):
    hi = s.astype(BF16)
    lo = (s - hi.astype(F32)).astype(BF16)
    return _dot(jnp.concatenate([hi, lo], axis=-1), exp_ref[...])


def _tmix_proj_kernel(*refs, rows_mode, has_vres, seq_blocks):
    it = iter(refs)
    h_ref = next(it)
    bnd_ref = next(it)
    shift_ref = None if rows_mode else next(it)
    vfirst_ref = next(it) if has_vres else None
    gn_ref = next(it)
    mu_ref = next(it)
    wrkv_ref = next(it)
    w0_ref, w1_ref, w2_ref = next(it), next(it), next(it)
    a0_ref, a1_ref, a2_ref = next(it), next(it), next(it)
    if has_vres:
        v0_ref, v1_ref, v2_ref = next(it), next(it), next(it)
    g1_ref, g2_ref = next(it), next(it)
    kk_ref, ka_ref, rk_ref = next(it), next(it), next(it)
    seg_ref, exp_ref = next(it), next(it)
    (r_out, k_out, v_out, lw_out, kk_out, bb_out, g_out, bonus_out, xl_out) = it

    gn = gn_ref[...]
    x = _rms(h_ref[0], gn)
    tm = x.shape[0]
    row = lax.broadcasted_iota(jnp.int32, x.shape, 0)
    rolled = pltpu.roll(x, 1, 0)
    if rows_mode:
        seq_len = seq_blocks
        x_prev = jnp.where(row % seq_len == 0, bnd_ref[0], rolled)
        xl_out[0] = x
    else:
        halo = bnd_ref[0]
        prev_row = _rms(halo[7:8, :], gn)
        first = pl.program_id(1) % seq_blocks == 0
        prev_row = jnp.where(first, shift_ref[0], prev_row)
        x_prev = jnp.where(row == 0, prev_row, rolled)
        xl_out[0] = x[tm - 1:tm, :]
    xx = x_prev - x

    def mixed(j):
        return (x + xx * mu_ref[j:j + 1, :]).astype(BF16)

    m_v = mixed(2)
    r = _dot(mixed(0), wrkv_ref[0])
    k = _dot(mixed(1), wrkv_ref[1])
    v = _dot(m_v, wrkv_ref[2])

    w_pre = w0_ref[...] + _dot(jnp.tanh(_dot(mixed(3), w1_ref[...])).astype(BF16), w2_ref[...])
    lw = jax.nn.sigmoid(w_pre) * (-math.exp(-0.5))
    a = jax.nn.sigmoid(a0_ref[...] + _dot(_dot(mixed(4), a1_ref[...]).astype(BF16), a2_ref[...]))
    g = _dot(jax.nn.sigmoid(_dot(mixed(5), g1_ref[...])).astype(BF16), g2_ref[...])
    if has_vres:
        gate = jax.nn.sigmoid(v0_ref[...] + _dot(_dot(m_v, v1_ref[...]).astype(BF16), v2_ref[...]))
        v = v + (vfirst_ref[0] - v) * gate

    kk = k * kk_ref[...]
    norm = jnp.maximum(jnp.sqrt(_seg_sum(kk * kk, seg_ref)), 1e-12)
    kk = kk * _seg_expand(1.0 / norm, exp_ref)
    k = k * (1.0 + (a - 1.0) * ka_ref[...])
    bonus = _seg_expand(_seg_sum(r * k * rk_ref[...], seg_ref), exp_ref) * v

    r_out[0] = r
    k_out[0] = k
    v_out[0] = v
    lw_out[0] = lw
    kk_out[0] = kk
    bb_out[0] = kk * a
    g_out[0] = g
    bonus_out[0] = bonus


def _tmix_proj(h, shift, vfirst, gn, p, seg, exp, *, tm):
    B, T, D = h.shape
    rows_mode = T < tm
    has_vres = vfirst is not None
    if rows_mode:
        nseq = tm // T
        assert (B * T) % tm == 0
        hb = h.reshape(B * T // tm, tm, D)
        bnd = jnp.zeros((B, T, D), F32).at[:, 0, :].set(shift).reshape(hb.shape)
        grid = (hb.shape[0], 1)
        blk = pl.BlockSpec((1, tm, D), lambda b, i: (b, 0, 0))
        in_arrays = [hb, bnd]
        in_specs = [blk, blk]
        seq_blocks = T
        if has_vres:
            in_arrays.append(vfirst.reshape(hb.shape))
            in_specs.append(blk)
        xl_shape = jax.ShapeDtypeStruct(hb.shape, F32)
        xl_spec = blk
        out_shape_main = hb.shape
    else:
        assert T % tm == 0 and tm % 8 == 0
        nblk = T // tm
        grid = (B, nblk)
        blk = pl.BlockSpec((1, tm, D), lambda b, i: (b, i, 0))
        halo = pl.BlockSpec((1, 8, D), lambda b, i: (b, jnp.maximum(i * (tm // 8) - 1, 0), 0))
        in_arrays = [h, h, shift.reshape(B, 1, D)]
        in_specs = [blk, halo, pl.BlockSpec((1, 1, D), lambda b, i: (b, 0, 0))]
        seq_blocks = nblk
        if has_vres:
            in_arrays.append(vfirst)
            in_specs.append(blk)
        xl_shape = jax.ShapeDtypeStruct((B, 1, D), F32)
        xl_spec = pl.BlockSpec((1, 1, D), lambda b, i: (b, 0, 0))
        out_shape_main = h.shape

    weights = [gn, p["mu"], p["w_rkv"], p["w0"], p["w1"], p["w2"], p["a0"], p["a1"], p["a2"]]
    if has_vres:
        weights += [p["v0"], p["v1"], p["v2"]]
    weights += [p["g1"], p["g2"], p["k_k"], p["k_a"], p["r_k"], seg, exp]
    in_arrays += weights
    in_specs += [_const_spec(w.shape) for w in weights]

    main = jax.ShapeDtypeStruct(out_shape_main, F32)
    outs = pl.pallas_call(
        functools.partial(_tmix_proj_kernel, rows_mode=rows_mode, has_vres=has_vres, seq_blocks=seq_blocks),
        grid=grid,
        in_specs=in_specs,
        out_specs=[blk] * 8 + [xl_spec],
        out_shape=[main] * 8 + [xl_shape],
        compiler_params=pltpu.CompilerParams(
            dimension_semantics=("parallel", "arbitrary"), vmem_limit_bytes=VMEM_LIMIT),
        name="tmix_proj",
    )(*in_arrays)
    main_outs = [o.reshape(B, T, D) for o in outs[:8]]
    if rows_mode:
        x_last = outs[8].reshape(B, T, D)[:, T - 1, :]
    else:
        x_last = outs[8].reshape(B, D)
    return main_outs, x_last


def _wkv_chunk_kernel(r_ref, k_ref, v_ref, lw_ref, kk_ref, bb_ref, s0_ref, o_ref, s_out_ref, s_scr, *,
                      heads, n_stage):
    c = pl.program_id(1)
    C = HEAD
    n_sub = r_ref.shape[1] // C
    W2 = 2 * HEAD

    @pl.when(c == 0)
    def _():
        s_scr[...] = s0_ref[0]

    lane = lax.broadcasted_iota(jnp.int32, (C, W2), 1)
    trow2 = lax.broadcasted_iota(jnp.int32, (C, W2), 0)
    s_idx = lane & (HEAD - 1)
    strict2 = trow2 > s_idx
    incl2 = trow2 >= s_idx
    even_lanes = lane < HEAD
    own = [even_lanes, jnp.logical_not(even_lanes)]
    eye_hi = jnp.where(trow2 == lane - HEAD, 1.0, 0.0)
    zero16 = jnp.zeros((C, W2), BF16)
    pairs = range(heads // 2)
    cols = [slice(p * W2, (p + 1) * W2) for p in pairs]
    hp = [(p, q) for p in pairs for q in range(2)]
    trow = lax.broadcasted_iota(jnp.int32, (C, r_ref.shape[2]), 0)

    zero_s = jnp.zeros((HEAD, W2), BF16)

    def phase_a(ci, sub):
        rows = slice(ci * C, (ci + 1) * C)
        lw = lw_ref[0, rows, :]
        cs = lw
        d = 1
        while d < C:
            cs = cs + jnp.where(trow >= d, pltpu.roll(cs, d, 0), 0.0)
            d *= 2
        p_incl = jnp.exp(cs)
        p_inv = jnp.exp(-cs)
        p_prev = jnp.exp(cs - lw)
        rt = (r_ref[0, rows, :] * p_incl).astype(BF16)
        nt = (kk_ref[0, rows, :] * (-p_prev)).astype(BF16)
        bt = (bb_ref[0, rows, :] * p_inv).astype(BF16)
        kt = (k_ref[0, rows, :] * p_inv).astype(BF16)
        vv = v_ref[0, rows, :].astype(BF16)
        sub.update(p_last=p_incl[C - 1:C, :], bt2=[bt[:, cl] for cl in cols], kt2=[kt[:, cl] for cl in cols],
                   vv2=[vv[:, cl] for cl in cols])
        sub["nr"] = [jnp.concatenate([jnp.where(own[q], nt[:, cols[p]], zero16),
                                      jnp.where(own[q], rt[:, cols[p]], zero16)], axis=0) for p, q in hp]
        g = [_dot_nt(sub["nr"][i], jnp.concatenate([sub["bt2"][p], sub["kt2"][p]], axis=0))
             for i, (p, q) in enumerate(hp)]
        yield
        gt = [jnp.where(strict2, gi[:C], 0.0) for gi in g]
        a_k = [pltpu.roll(gti, HEAD, 1)[:, :HEAD].astype(BF16) for gti in gt]
        sub["av"] = [_dot(a_k[i], sub["vv2"][p]) for i, (p, q) in enumerate(hp)]
        z = [jnp.where(even_lanes, gti, eye_hi) for gti in gt]
        sub["lo"] = [jnp.where(incl2, gi[C:], 0.0).astype(BF16) for gi in g]
        for _ in range(n_stage):
            yield
            zb = [zi.astype(BF16) for zi in z]
            res = [_dot(zbi[:, :HEAD], zbi) for zbi in zb]
            z = [jnp.where(even_lanes, ri, zi + ri) for zi, ri in zip(z, res)]
        sub["t16"] = [pltpu.roll(zi, HEAD, 1)[:, :HEAD].astype(BF16) for zi in z]

    def phase_b(ci, sub, carry):
        rows = slice(ci * C, (ci + 1) * C)
        state = carry["state"]
        s16 = [s.astype(BF16) for s in state]
        x0 = [_dot_nt(sub["nr"][i], jnp.concatenate([s16[p], zero_s] if q == 0 else [zero_s, s16[p]], axis=0))
              for i, (p, q) in enumerate(hp)]
        yield
        w16 = [(x0[i][:C] + sub["av"][i]).astype(BF16) for i in range(len(hp))]
        u16 = [_dot(sub["t16"][i], w16[i]).astype(BF16) for i in range(len(hp))]
        yield
        new_state = []
        for p in pairs:
            lhs = jnp.concatenate([u16[2 * p], u16[2 * p + 1], sub["vv2"][p]], axis=0)
            rhs = jnp.concatenate([jnp.where(own[0], sub["bt2"][p], zero16),
                                   jnp.where(own[1], sub["bt2"][p], zero16), sub["kt2"][p]], axis=0)
            upd = _dot_tn(lhs, rhs)
            upd = jnp.where(even_lanes, upd[:HEAD], upd[HEAD:])
            new_state.append((state[p] + upd) * sub["p_last"][:, cols[p]])
        carry["state"] = new_state
        yield
        outs = [x0[i][C:] + _dot(sub["lo"][i], jnp.concatenate([u16[i], sub["vv2"][p]], axis=0))
                for i, (p, q) in enumerate(hp)]
        for p in pairs:
            o_ref[0, rows, cols[p]] = jnp.where(even_lanes, outs[2 * p], outs[2 * p + 1])

    def run_interleaved(*gens):
        live = list(gens)
        while live:
            for gen in list(live):
                try:
                    next(gen)
                except StopIteration:
                    live.remove(gen)

    carry = dict(state=[s_scr[p] for p in pairs])
    subs = [dict() for _ in range(n_sub)]
    run_interleaved(*[phase_a(ci, subs[ci]) for ci in range(n_sub)])
    for ci in range(n_sub):
        run_interleaved(phase_b(ci, subs[ci], carry))
    for p in pairs:
        s_scr[p] = carry["state"][p]

    @pl.when(c == pl.num_programs(1) - 1)
    def _():
        s_out_ref[0] = s_scr[...]


def _wkv_chunked(r, k, v, lw, kk, bb, s0):
    B, T, D = r.shape
    heads = D // HEAD
    chunk = WKV_CHUNK
    assert chunk == HEAD and heads % 2 == 0
    Tp = -(-T // chunk) * chunk
    if Tp != T:
        r, k, v, lw, kk, bb = (jnp.pad(a, ((0, 0), (0, Tp - T), (0, 0))) for a in (r, k, v, lw, kk, bb))
    pair = lambda s: s.reshape(B, heads // 2, 2, HEAD, HEAD).transpose(0, 1, 3, 2, 4).reshape(
        B, heads // 2, HEAD, 2 * HEAD)
    unpair = lambda s: s.reshape(B, heads // 2, HEAD, 2, HEAD).transpose(0, 1, 3, 2, 4).reshape(
        B, heads, HEAD, HEAD)
    rows = chunk * WKV_SUB if Tp % (chunk * WKV_SUB) == 0 else chunk
    blk = pl.BlockSpec((1, rows, D), lambda b, c: (b, c, 0))
    st = pl.BlockSpec((1, heads // 2, HEAD, 2 * HEAD), lambda b, c: (b, 0, 0, 0))
    o, s_out = pl.pallas_call(
        functools.partial(_wkv_chunk_kernel, heads=heads,
                          n_stage=(min(T, chunk) - 1).bit_length() if Tp == chunk else chunk.bit_length() - 1),
        grid=(B, Tp // rows),
        in_specs=[blk] * 6 + [st],
        out_specs=[blk, st],
        out_shape=[jax.ShapeDtypeStruct((B, Tp, D), F32),
                   jax.ShapeDtypeStruct((B, heads // 2, HEAD, 2 * HEAD), F32)],
        scratch_shapes=[pltpu.VMEM((heads // 2, HEAD, 2 * HEAD), F32)],
        compiler_params=pltpu.CompilerParams(
            dimension_semantics=("parallel", "arbitrary"), vmem_limit_bytes=VMEM_LIMIT),
        name="wkv_chunk",
    )(r, k, v, lw, kk, bb, pair(s0))
    return o[:, :T], unpair(s_out)


def _qkv_kernel(h_ref, gn_ref, w_ref, q_out, kp_out, vp_out, kl_out, vl_out, *, npad, last_from):
    i = pl.program_id(1)
    D = h_ref.shape[2]

    @pl.when(i < npad)
    def _():
        kp_out[0] = jnp.zeros(kp_out.shape[1:], kp_out.dtype)
        vp_out[0] = jnp.zeros(vp_out.shape[1:], vp_out.dtype)

    @pl.when(i >= npad)
    def _():
        xn = _rms(h_ref[0], gn_ref[...]).astype(BF16)
        q = _dot(xn, w_ref[:, 0:D])
        k = _dot(xn, w_ref[:, D:2 * D])
        v = _dot(xn, w_ref[:, 2 * D:3 * D])
        q_out[0] = (q * (HEAD ** -0.5 * LOG2E)).astype(BF16)
        kp_out[0] = k.astype(BF16)
        vp_out[0] = v.astype(BF16)

        @pl.when(i >= npad + last_from)
        def _():
            kl_out[0] = k
            vl_out[0] = v


def _qkv_proj(h, gn, w_qkv, *, tq, pad_rows, keep):
    B, T, D = h.shape
    assert T % tq == 0 and pad_rows % tq == 0
    npad = pad_rows // tq
    assert keep % tq == 0
    last_from = (T - keep) // tq
    blk_in = pl.BlockSpec((1, tq, D), lambda b, i: (b, jnp.maximum(i - npad, 0), 0))
    blk_pad = pl.BlockSpec((1, tq, D), lambda b, i: (b, i, 0))
    blk_last = pl.BlockSpec((1, tq, D), lambda b, i: (b, jnp.maximum(i - npad - last_from, 0), 0))
    return pl.pallas_call(
        functools.partial(_qkv_kernel, npad=npad, last_from=last_from),
        grid=(B, npad + T // tq),
        in_specs=[blk_in, _const_spec(gn.shape), _const_spec(w_qkv.shape)],
        out_specs=[blk_in, blk_pad, blk_pad, blk_last, blk_last],
        out_shape=[jax.ShapeDtypeStruct((B, T, D), BF16),
                   jax.ShapeDtypeStruct((B, pad_rows + T, D), BF16),
                   jax.ShapeDtypeStruct((B, pad_rows + T, D), BF16),
                   jax.ShapeDtypeStruct((B, keep, D), F32),
                   jax.ShapeDtypeStruct((B, keep, D), F32)],
        compiler_params=pltpu.CompilerParams(
            dimension_semantics=("parallel", "arbitrary"), vmem_limit_bytes=VMEM_LIMIT),
        name="qkv_proj",
    )(h, gn, w_qkv)


def _toeplitz(line, rows):
    return pltpu.roll(jnp.broadcast_to(line, (rows, line.shape[1])), 0, 1, stride=1, stride_axis=0)


def _band_attn_kernel(q_ref, k_ref, v_ref, line_ref, o_ref, tbl_ref, *, nk, pad_rows):
    mq = q_ref.shape[1]
    W2 = 2 * HEAD
    n_cols = q_ref.shape[2] // W2
    c = pl.program_id(2)

    @pl.when((pl.program_id(1) == 0) & (c == 0))
    def _():
        qc = lax.broadcasted_iota(jnp.int32, (mq, nk), 0) // CHUNK
        kc = lax.broadcasted_iota(jnp.int32, (mq, nk), 1) // CHUNK
        in_band = (kc >= qc) & (kc <= qc + LEFT_CHUNKS)
        for i in range(tbl_ref.shape[0]):
            tbl_ref[i] = jnp.where(in_band, _toeplitz(line_ref[i], mq)[:, :nk], NEG_INF)

    start = pl.multiple_of(c * mq, mq)
    lane = lax.broadcasted_iota(jnp.int32, (mq, W2), 1)
    even_lanes = lane < HEAD
    own = [even_lanes, jnp.logical_not(even_lanes)]
    cols = [slice(p * W2, (p + 1) * W2) for p in range(n_cols)]
    hp = [(p, q) for p in range(n_cols) for q in range(2)]

    def attend(band_has_padding):
        kb = [k_ref[0, pl.ds(start, nk), cl] for cl in cols]
        vb = [v_ref[0, pl.ds(start, nk), cl] for cl in cols]
        q2 = [q_ref[0, :, cl] for cl in cols]
        s = [_dot_nt(jnp.where(own[q], q2[p], jnp.zeros_like(q2[p])), kb[p]) + tbl_ref[i]
             for i, (p, q) in enumerate(hp)]
        if band_has_padding:
            col = lax.broadcasted_iota(jnp.int32, (1, nk), 1)
            pen = jnp.where(col >= pad_rows - c * mq, 0.0, NEG_INF)
            s = [si + pen for si in s]
        m = [jnp.max(si, axis=-1, keepdims=True) for si in s]
        pr = [jnp.exp2(si - mi) for si, mi in zip(s, m)]
        l = [jnp.sum(pi, axis=-1, keepdims=True) for pi in pr]
        o = [_dot(pr[i].astype(BF16), vb[p]) * (1.0 / l[i]) for i, (p, q) in enumerate(hp)]
        for p in range(n_cols):
            o_ref[0, :, cols[p]] = jnp.where(even_lanes, o[2 * p], o[2 * p + 1]).astype(o_ref.dtype)

    pl.when(c * mq < pad_rows)(lambda: attend(True))
    pl.when(c * mq >= pad_rows)(lambda: attend(False))


def _band_attention(q, kpad, vpad, lines, *, mq, nk, pad_rows, heads_per_step):
    B, T, D = q.shape
    Tk = kpad.shape[1]
    width = heads_per_step * HEAD
    assert T % mq == 0 and D % width == 0 and width % (2 * HEAD) == 0
    q_spec = pl.BlockSpec((1, mq, width), lambda g, b, c: (b, c, g))
    kv_spec = pl.BlockSpec((1, Tk, width), lambda g, b, c: (b, 0, g))
    line_spec = pl.BlockSpec((heads_per_step, 1, lines.shape[2]), lambda g, b, c: (g, 0, 0))
    return pl.pallas_call(
        functools.partial(_band_attn_kernel, nk=nk, pad_rows=pad_rows),
        grid=(D // width, B, T // mq),
        in_specs=[q_spec, kv_spec, kv_spec, line_spec],
        out_specs=q_spec,
        out_shape=jax.ShapeDtypeStruct((B, T, D), BF16),
        scratch_shapes=[pltpu.VMEM((heads_per_step, mq, nk), F32)],
        compiler_params=pltpu.CompilerParams(
            dimension_semantics=("arbitrary", "arbitrary", "arbitrary"), vmem_limit_bytes=VMEM_LIMIT),
        name="band_attn",
    )(q, kpad, vpad, lines)


def _sample_attn_kernel(q_ref, kc_ref, vc_ref, kn_ref, vn_ref, line_ref, o_ref, tc_ref, tn_ref, *, heads):
    W2 = 2 * HEAD
    tq = q_ref.shape[1]
    nc = kc_ref.shape[1]

    @pl.when(pl.program_id(0) == 0)
    def _():
        for i in range(heads):
            t = _toeplitz(line_ref[i], tq)
            tc_ref[i] = t[:, :nc]
            tn_ref[i] = t[:, nc:nc + tq]

    lane = lax.broadcasted_iota(jnp.int32, (tq, W2), 1)
    even_lanes = lane < HEAD
    own = [even_lanes, jnp.logical_not(even_lanes)]
    hp = [(p, q) for p in range(heads // 2) for q in range(2)]
    cols = [slice(p * W2, (p + 1) * W2) for p in range(heads // 2)]
    q2 = [q_ref[0, :, cl] for cl in cols]
    qm = [jnp.where(own[q], q2[p], jnp.zeros_like(q2[p])) for p, q in hp]
    s_c = [_dot_nt(qm[i], kc_ref[0, :, cols[p]]) + tc_ref[i] for i, (p, q) in enumerate(hp)]
    s_n = [_dot_nt(qm[i], kn_ref[0, :, cols[p]]) + tn_ref[i] for i, (p, q) in enumerate(hp)]
    m = [jnp.maximum(jnp.max(a, axis=-1, keepdims=True), jnp.max(b, axis=-1, keepdims=True))
         for a, b in zip(s_c, s_n)]
    p_c = [jnp.exp2(a - mi) for a, mi in zip(s_c, m)]
    p_n = [jnp.exp2(b - mi) for b, mi in zip(s_n, m)]
    l = [jnp.sum(a, axis=-1, keepdims=True) + jnp.sum(b, axis=-1, keepdims=True) for a, b in zip(p_c, p_n)]
    o = [(_dot(p_c[i].astype(BF16), vc_ref[0, :, cols[p]]) + _dot(p_n[i].astype(BF16), vn_ref[0, :, cols[p]]))
         * (1.0 / l[i]) for i, (p, q) in enumerate(hp)]
    for p in range(heads // 2):
        o_ref[0, :, cols[p]] = jnp.where(even_lanes, o[2 * p], o[2 * p + 1]).astype(o_ref.dtype)


def _sample_attention(q, kc, vc, kn, vn, lines):
    B, T, D = q.shape
    W = kc.shape[1]
    heads = D // HEAD
    new = pl.BlockSpec((1, T, D), lambda b: (b, 0, 0))
    old = pl.BlockSpec((1, W, D), lambda b: (b, 0, 0))
    return pl.pallas_call(
        functools.partial(_sample_attn_kernel, heads=heads),
        grid=(B,),
        in_specs=[new, old, old, new, new, _const_spec(lines.shape)],
        out_specs=new,
        out_shape=jax.ShapeDtypeStruct((B, T, D), BF16),
        scratch_shapes=[pltpu.VMEM((heads, T, W), F32), pltpu.VMEM((heads, T, T), F32)],
        compiler_params=pltpu.CompilerParams(dimension_semantics=("arbitrary",), vmem_limit_bytes=VMEM_LIMIT),
        name="sample_attn",
    )(q, kc, vc, kn, vn, lines)


def _bias_lines(rel_bias, mq, nk):
    period = -(-(nk + mq) // LANES) * LANES
    e = jnp.arange(period, dtype=jnp.int32)
    e = jnp.where(e < nk, e, e - period)
    rel = jnp.clip(ATT_WINDOW - e, -(CHUNK - 1), REL_MAX_PAST) + (CHUNK - 1)
    return (rel_bias.astype(F32)[:, rel] * LOG2E)[:, None, :]


def _mix_ffn_kernel(*refs, rwkv, final, n_ff_chunks):
    it = iter(refs)
    h_ref = next(it)
    if rwkv:
        o_ref, bonus_ref, g_ref = next(it), next(it), next(it)
        lnw_ref, lnb_ref, seg_ref, exp_ref = next(it), next(it), next(it), next(it)
    else:
        a_ref = next(it)
    wo_ref, gf_ref, wgu_ref, wd_ref = next(it), next(it), next(it), next(it)
    gfin_ref = next(it) if final else None
    out_ref = next(it)

    if rwkv:
        o = o_ref[...]
        mean = _seg_expand(_seg_sum(o, seg_ref) * (1.0 / HEAD), exp_ref)
        d = o - mean
        rstd = lax.rsqrt(_seg_sum(d * d, seg_ref) * (1.0 / HEAD) + LNX_EPS)
        y = d * _seg_expand(rstd, exp_ref) * lnw_ref[...] + lnb_ref[...]
        a = ((y + bonus_ref[...]) * g_ref[...]).astype(BF16)
    else:
        a = a_ref[...]
    h1 = h_ref[...] + _dot(a, wo_ref[...])

    xn = _rms(h1, gf_ref[...]).astype(BF16)
    dff = wd_ref.shape[0]
    fc = dff // n_ff_chunks
    acc = h1
    for c in range(n_ff_chunks):
        gate = _dot(xn, wgu_ref[:, c * fc:(c + 1) * fc])
        up = _dot(xn, wgu_ref[:, dff + c * fc:dff + (c + 1) * fc])
        act = (gate * jax.nn.sigmoid(gate) * up).astype(BF16)
        acc = acc + _dot(act, wd_ref[c * fc:(c + 1) * fc, :])
    if final:
        acc = _rms(acc, gfin_ref[...])
    out_ref[...] = acc


def _mix_ffn(h, mix_inputs, mix_weights, w_out, gf, w_gu, w_down, g_final, *, rwkv, tm, n_ff_chunks):
    N, D = h.shape
    assert N % tm == 0
    final = g_final is not None
    row = pl.BlockSpec((tm, D), lambda i: (i, 0))
    consts = list(mix_weights) + [w_out, gf, w_gu, w_down] + ([g_final] if final else [])
    return pl.pallas_call(
        functools.partial(_mix_ffn_kernel, rwkv=rwkv, final=final, n_ff_chunks=n_ff_chunks),
        grid=(N // tm,),
        in_specs=[row] + [row] * len(mix_inputs) + [_const_spec(w.shape) for w in consts],
        out_specs=row,
        out_shape=jax.ShapeDtypeStruct((N, D), F32),
        compiler_params=pltpu.CompilerParams(
            dimension_semantics=("parallel",), vmem_limit_bytes=VMEM_LIMIT),
        name="mix_ffn_rwkv" if rwkv else "mix_ffn_attn",
    )(h, *mix_inputs, *consts)


def _row_block(n, want):
    t = min(want, n)
    while n % t:
        t //= 2
    return t


def kernel(x_prompt, x_sample, state_wkv, state_shift, cache_k, cache_v, norm_mix, norm_ffn, norm_final, rwkv_mu, rwkv_w_rkv, rwkv_w_out, rwkv_decay_w0, rwkv_decay_w1, rwkv_decay_w2, rwkv_iclr_a0, rwkv_iclr_a1, rwkv_iclr_a2, rwkv_vres_v0, rwkv_vres_v1, rwkv_vres_v2, rwkv_gate_g1, rwkv_gate_g2, rwkv_k_k, rwkv_k_a, rwkv_r_k, rwkv_lnx_w, rwkv_lnx_b, attn_w_qkv, attn_w_out, attn_rel_bias, ffn_w_gu, ffn_w_down):
    B, T, D = x_prompt.shape
    Bs, Ts, _ = x_sample.shape
    depth = norm_mix.shape[0]
    heads = D // HEAD
    W = cache_k.shape[2]
    assert W == ATT_WINDOW and Ts <= CHUNK and PAST_LEN % CHUNK == 0

    vec = lambda a: a.reshape(1, D).astype(F32)
    bf = lambda a: a.astype(BF16)
    split_heads = lambda xs, b: jnp.stack(xs).reshape(len(xs), b, -1, heads, HEAD)

    col = jnp.arange(D, dtype=jnp.int32)[:, None] // HEAD
    seg = (col == jnp.arange(SEG_LANES, dtype=jnp.int32)[None, :]).astype(BF16)
    exp = jnp.concatenate([seg.T, seg.T], axis=0)

    hp, hs = x_prompt, x_sample
    vf_p = vf_s = None
    wkv_p, shift_p, k_p, v_p = [], [], [], []
    wkv_s, shift_s, k_s, v_s = [], [], [], []

    tm_proj = _row_block(T, 256)
    tm_ffn_p = _row_block(B * T, 256)
    tm_ffn_pa = _row_block(B * T, 512)
    tm_ffn_s = _row_block(Bs * Ts, 512)
    tq = _row_block(min(T, ATT_WINDOW), 512)
    group = _row_block(T, 4 * CHUNK)
    nk_p = ATT_WINDOW + group

    for layer in range(depth):
        j = layer // 2
        gn = vec(norm_mix[layer])
        last = layer == depth - 1
        ffn_args = dict(gf=vec(norm_ffn[layer]), w_gu=bf(ffn_w_gu[layer]), w_down=bf(ffn_w_down[layer]),
                        g_final=vec(norm_final) if last else None, n_ff_chunks=1)
        if layer % 2 == 0:
            p = dict(mu=jnp.pad(rwkv_mu[j].astype(F32), ((0, 2), (0, 0))), w_rkv=bf(rwkv_w_rkv[j]),
                     w0=vec(rwkv_decay_w0[j]), w1=bf(rwkv_decay_w1[j]), w2=bf(rwkv_decay_w2[j]),
                     a0=vec(rwkv_iclr_a0[j]), a1=bf(rwkv_iclr_a1[j]), a2=bf(rwkv_iclr_a2[j]),
                     g1=bf(rwkv_gate_g1[j]), g2=bf(rwkv_gate_g2[j]),
                     k_k=vec(rwkv_k_k[j]), k_a=vec(rwkv_k_a[j]), r_k=vec(rwkv_r_k[j]))
            if j > 0:
                p.update(v0=vec(rwkv_vres_v0[j - 1]), v1=bf(rwkv_vres_v1[j - 1]), v2=bf(rwkv_vres_v2[j - 1]))
            lnw, lnb, w_out = vec(rwkv_lnx_w[j]), vec(rwkv_lnx_b[j]), bf(rwkv_w_out[j])

            def rwkv_side(h, shift, s0, vfirst, tm_proj, tm_ffn):
                b_, t_, _ = h.shape
                (r, k, v, lw, kk, bb, g, bonus), x_last = _tmix_proj(
                    h, shift, vfirst, gn, p, seg, exp, tm=tm_proj)
                o, s_new = _wkv_chunked(r, k, v, lw, kk, bb, s0)
                rows = lambda a: a.reshape(b_ * t_, D)
                h_new = _mix_ffn(rows(h), (rows(o), rows(bonus), rows(g)), (lnw, lnb, seg, exp), w_out,
                                 rwkv=True, tm=tm_ffn, **ffn_args)
                return h_new.reshape(h.shape), x_last, s_new, (v if vfirst is None else vfirst)

            hp, sh, st, vf_p = rwkv_side(hp, jnp.zeros((B, D), F32), jnp.zeros((B, heads, HEAD, HEAD), F32),
                                         vf_p, tm_proj, tm_ffn_p)
            wkv_p.append(st); shift_p.append(sh)
            hs, sh, st, vf_s = rwkv_side(hs, state_shift[j], state_wkv[j].astype(F32),
                                         vf_s, tm_ffn_s, tm_ffn_s)
            wkv_s.append(st); shift_s.append(sh)
        else:
            w_qkv, w_out = bf(attn_w_qkv[j]), bf(attn_w_out[j])
            q, kpad, vpad, k_last, v_last = _qkv_proj(hp, gn, w_qkv, tq=tq, pad_rows=ATT_WINDOW,
                                                      keep=min(ATT_WINDOW, T))
            att = _band_attention(q, kpad, vpad, _bias_lines(attn_rel_bias[j], group, nk_p), mq=group, nk=nk_p,
                                  pad_rows=ATT_WINDOW, heads_per_step=8)
            hp = _mix_ffn(hp.reshape(B * T, D), (att.reshape(B * T, D),), (), w_out,
                          rwkv=False, tm=tm_ffn_pa, **ffn_args).reshape(B, T, D)
            k_p.append(k_last); v_p.append(v_last)
            ns = Bs * Ts
            q, k16, v16, k_new, v_new = _qkv_proj(hs.reshape(1, ns, D), gn, w_qkv, tq=tm_ffn_s, pad_rows=0, keep=ns)
            att = _sample_attention(q.reshape(Bs, Ts, D), bf(cache_k[j]).reshape(Bs, W, D),
                                    bf(cache_v[j]).reshape(Bs, W, D), k16.reshape(Bs, Ts, D),
                                    v16.reshape(Bs, Ts, D), _bias_lines(attn_rel_bias[j], Ts, W + Ts))
            hs = _mix_ffn(hs.reshape(ns, D), (att.reshape(ns, D),), (), w_out,
                          rwkv=False, tm=tm_ffn_s, **ffn_args).reshape(Bs, Ts, D)
            k_s.append(k_new); v_s.append(v_new)

    return (hp, hs,
            jnp.stack(wkv_p), jnp.stack(shift_p), split_heads(k_p, B), split_heads(v_p, B),
            jnp.stack(wkv_s), jnp.stack(shift_s), split_heads(k_s, Bs), split_heads(v_s, Bs))
```

```python
import functools

import jax
import jax.numpy as jnp
from jax import lax
from jax.experimental import pallas as pl
from jax.experimental.pallas import tpu as pltpu

F32 = jnp.float32
BF16 = jnp.bfloat16

HEAD = 64
CHUNK = 64
LEFT_CHUNKS = 8
ATT_WINDOW = LEFT_CHUNKS * CHUNK
REL_MAX_PAST = 256
PAST_LEN = 4096
RMS_EPS = 1e-6
LNX_EPS = 64e-5
NEG_INF = -1e30
LOG2E = 1.4426950408889634
WKV_CHUNK = 64
WKV_SUB = 2
SEG_LANES = 128
VMEM_LIMIT = 56 * 1024 * 1024


def _dot(a, b):
    return jnp.dot(a, b, preferred_element_type=F32)


def _dot_nt(a, b):
    return lax.dot_general(a, b, (((1,), (1,)), ((), ())), preferred_element_type=F32)


def _dot_tn(a, b):
    return lax.dot_general(a, b, (((0,), (0,)), ((), ())), preferred_element_type=F32)


def _rms(x, g):
    return x * lax.rsqrt(jnp.mean(x * x, axis=-1, keepdims=True) + RMS_EPS) * g


def _const_spec(shape):
    nd = len(shape)
    return pl.BlockSpec(shape, lambda *_: (0,) * nd, pipeline_mode=pl.Buffered(1))


def _seg_sum(x, seg_ref):
    return _dot(x.astype(BF16), seg_ref[...])


def _seg_expand(s, exp_ref):
    hi = s.astype(BF16)
    lo = (s - hi.astype(F32)).astype(BF16)
    return _dot(jnp.concatenate([hi, lo], axis=-1), exp_ref[...])


def _tmix_proj_kernel(*refs, rows_mode, has_vres, seq_blocks):
    it = iter(refs)
    h_ref = next(it)
    bnd_ref = next(it)
    shift_ref = None if rows_mode else next(it)
    vfirst_ref = next(it) if has_vres else None
    gn_ref = next(it)
    mu_ref = next(it)
    wrkv_ref = next(it)
    w0_ref, w1_ref, w2_ref = next(it), next(it), next(it)
    a0_ref, a1_ref, a2_ref = next(it), next(it), next(it)
    if has_vres:
        v0_ref, v1_ref, v2_ref = next(it), next(it), next(it)
    g1_ref, g2_ref = next(it), next(it)
    kk_ref, ka_ref, rk_ref = next(it), next(it), next(it)
    seg_ref, exp_ref = next(it), next(it)
    (r_out, k_out, v_out, lw_out, kk_out, bb_out, g_out, bonus_out, xl_out) = it

    gn = gn_ref[...]
    x = _rms(h_ref[0], gn)
    tm = x.shape[0]
    row = lax.broadcasted_iota(jnp.int32, x.shape, 0)
    rolled = pltpu.roll(x, 1, 0)
    if rows_mode:
        seq_len = seq_blocks
        x_prev = jnp.where(row % seq_len == 0, bnd_ref[0], rolled)
        xl_out[0] = x
    else:
        halo = bnd_ref[0]
        prev_row = _rms(halo[7:8, :], gn)
        first = pl.program_id(1) % seq_blocks == 0
        prev_row = jnp.where(first, shift_ref[0], prev_row)
        x_prev = jnp.where(row == 0, prev_row, rolled)
        xl_out[0] = x[tm - 1:tm, :]
    xx = x_prev - x

    def mixed(j):
        return (x + xx * mu_ref[j:j + 1, :]).astype(BF16)

    m_v = mixed(2)
    r = _dot(mixed(0), wrkv_ref[0])
    k = _dot(mixed(1), wrkv_ref[1])
    v = _dot(m_v, wrkv_ref[2])

    w_pre = w0_ref[...] + _dot(jnp.tanh(_dot(mixed(3), w1_ref[...])).astype(BF16), w2_ref[...])
    lw = jax.nn.sigmoid(w_pre) * (-0.6065306597126334)
    a = jax.nn.sigmoid(a0_ref[...] + _dot(_dot(mixed(4), a1_ref[...]).astype(BF16), a2_ref[...]))
    g = _dot(jax.nn.sigmoid(_dot(mixed(5), g1_ref[...])).astype(BF16), g2_ref[...])
    if has_vres:
        gate = jax.nn.sigmoid(v0_ref[...] + _dot(_dot(m_v, v1_ref[...]).astype(BF16), v2_ref[...]))
        v = v + (vfirst_ref[0] - v) * gate

    kk = k * kk_ref[...]
    norm = jnp.maximum(jnp.sqrt(_seg_sum(kk * kk, seg_ref)), 1e-12)
    kk = kk * _seg_expand(1.0 / norm, exp_ref)
    k = k * (1.0 + (a - 1.0) * ka_ref[...])
    bonus = _seg_expand(_seg_sum(r * k * rk_ref[...], seg_ref), exp_ref) * v

    r_out[0] = r
    k_out[0] = k
    v_out[0] = v
    lw_out[0] = lw
    kk_out[0] = kk
    bb_out[0] = kk * a
    g_out[0] = g
    bonus_out[0] = bonus


def _tmix_proj(h, shift, vfirst, gn, p, seg, exp, *, tm):
    B, T, D = h.shape
    rows_mode = T < tm
    has_vres = vfirst is not None
    if rows_mode:
        nseq = tm // T
        assert (B * T) % tm == 0
        hb = h.reshape(B * T // tm, tm, D)
        bnd = jnp.zeros((B, T, D), F32).at[:, 0, :].set(shift).reshape(hb.shape)
        grid = (hb.shape[0], 1)
        blk = pl.BlockSpec((1, tm, D), lambda b, i: (b, 0, 0))
        in_arrays = [hb, bnd]
        in_specs = [blk, blk]
        seq_blocks = T
        if has_vres:
            in_arrays.append(vfirst.reshape(hb.shape))
            in_specs.append(blk)
        xl_shape = jax.ShapeDtypeStruct(hb.shape, F32)
        xl_spec = blk
        out_shape_main = hb.shape
    else:
        assert T % tm == 0 and tm % 8 == 0
        nblk = T // tm
        grid = (B, nblk)
        blk = pl.BlockSpec((1, tm, D), lambda b, i: (b, i, 0))
        halo = pl.BlockSpec((1, 8, D), lambda b, i: (b, jnp.maximum(i * (tm // 8) - 1, 0), 0))
        in_arrays = [h, h, shift.reshape(B, 1, D)]
        in_specs = [blk, halo, pl.BlockSpec((1, 1, D), lambda b, i: (b, 0, 0))]
        seq_blocks = nblk
        if has_vres:
            in_arrays.append(vfirst)
            in_specs.append(blk)
        xl_shape = jax.ShapeDtypeStruct((B, 1, D), F32)
        xl_spec = pl.BlockSpec((1, 1, D), lambda b, i: (b, 0, 0))
        out_shape_main = h.shape

    weights = [gn, p["mu"], p["w_rkv"], p["w0"], p["w1"], p["w2"], p["a0"], p["a1"], p["a2"]]
    if has_vres:
        weights += [p["v0"], p["v1"], p["v2"]]
    weights += [p["g1"], p["g2"], p["k_k"], p["k_a"], p["r_k"], seg, exp]
    in_arrays += weights
    in_specs += [_const_spec(w.shape) for w in weights]

    main = jax.ShapeDtypeStruct(out_shape_main, F32)
    outs = pl.pallas_call(
        functools.partial(_tmix_proj_kernel, rows_mode=rows_mode, has_vres=has_vres, seq_blocks=seq_blocks),
        grid=grid,
        in_specs=in_specs,
        out_specs=[blk] * 8 + [xl_spec],
        out_shape=[main] * 8 + [xl_shape],
        compiler_params=pltpu.CompilerParams(
            dimension_semantics=("parallel", "arbitrary"), vmem_limit_bytes=VMEM_LIMIT),
        name="tmix_proj",
    )(*in_arrays)
    main_outs = [o.reshape(B, T, D) for o in outs[:8]]
    if rows_mode:
        x_last = outs[8].reshape(B, T, D)[:, T - 1, :]
    else:
        x_last = outs[8].reshape(B, D)
    return main_outs, x_last


def _wkv_chunk_kernel(r_ref, k_ref, v_ref, lw_ref, kk_ref, bb_ref, s0_ref, o_ref, s_out_ref, s_scr, *,
                      heads, n_stage):
    c = pl.program_id(1)
    C = HEAD
    n_sub = r_ref.shape[1] // C
    W2 = 2 * HEAD

    @pl.when(c == 0)
    def _():
        s_scr[...] = s0_ref[0]

    lane = lax.broadcasted_iota(jnp.int32, (C, W2), 1)
    trow2 = lax.broadcasted_iota(jnp.int32, (C, W2), 0)
    s_idx = lane & (HEAD - 1)
    strict2 = trow2 > s_idx
    incl2 = trow2 >= s_idx
    even_lanes = lane < HEAD
    own = [even_lanes, jnp.logical_not(even_lanes)]
    eye_hi = jnp.where(trow2 == lane - HEAD, 1.0, 0.0)
    zero16 = jnp.zeros((C, W2), BF16)
    pairs = range(heads // 2)
    cols = [slice(p * W2, (p + 1) * W2) for p in pairs]
    hp = [(p, q) for p in pairs for q in range(2)]
    trow = lax.broadcasted_iota(jnp.int32, (C, r_ref.shape[2]), 0)

    zero_s = jnp.zeros((HEAD, W2), BF16)

    def phase_a(ci, sub):
        rows = slice(ci * C, (ci + 1) * C)
        lw = lw_ref[0, rows, :]
        cs = lw
        d = 1
        while d < C:
            cs = cs + jnp.where(trow >= d, pltpu.roll(cs, d, 0), 0.0)
            d *= 2
        p_incl = jnp.exp(cs)
        p_inv = jnp.exp(-cs)
        p_prev = jnp.exp(cs - lw)
        rt = (r_ref[0, rows, :] * p_incl).astype(BF16)
        nt = (kk_ref[0, rows, :] * (-p_prev)).astype(BF16)
        bt = (bb_ref[0, rows, :] * p_inv).astype(BF16)
        kt = (k_ref[0, rows, :] * p_inv).astype(BF16)
        vv = v_ref[0, rows, :].astype(BF16)
        sub.update(p_last=p_incl[C - 1:C, :], bt2=[bt[:, cl] for cl in cols], kt2=[kt[:, cl] for cl in cols],
                   vv2=[vv[:, cl] for cl in cols])
        sub["nr"] = [jnp.concatenate([jnp.where(own[q], nt[:, cols[p]], zero16),
                                      jnp.where(own[q], rt[:, cols[p]], zero16)], axis=0) for p, q in hp]
        g = [_dot_nt(sub["nr"][i], jnp.concatenate([sub["bt2"][p], sub["kt2"][p]], axis=0))
             for i, (p, q) in enumerate(hp)]
        yield
        gt = [jnp.where(strict2, gi[:C], 0.0) for gi in g]
        a_k = [pltpu.roll(gti, HEAD, 1)[:, :HEAD].astype(BF16) for gti in gt]
        sub["av"] = [_dot(a_k[i], sub["vv2"][p]) for i, (p, q) in enumerate(hp)]
        z = [jnp.where(even_lanes, gti, eye_hi) for gti in gt]
        sub["lo"] = [jnp.where(incl2, gi[C:], 0.0).astype(BF16) for gi in g]
        for _ in range(n_stage):
            yield
            zb = [zi.astype(BF16) for zi in z]
            res = [_dot(zbi[:, :HEAD], zbi) for zbi in zb]
            z = [jnp.where(even_lanes, ri, zi + ri) for zi, ri in zip(z, res)]
        sub["t16"] = [pltpu.roll(zi, HEAD, 1)[:, :HEAD].astype(BF16) for zi in z]

    def phase_b(ci, sub, carry):
        rows = slice(ci * C, (ci + 1) * C)
        state = carry["state"]
        s16 = [s.astype(BF16) for s in state]
        x0 = [_dot_nt(sub["nr"][i], jnp.concatenate([s16[p], zero_s] if q == 0 else [zero_s, s16[p]], axis=0))
              for i, (p, q) in enumerate(hp)]
        yield
        w16 = [(x0[i][:C] + sub["av"][i]).astype(BF16) for i in range(len(hp))]
        u16 = [_dot(sub["t16"][i], w16[i]).astype(BF16) for i in range(len(hp))]
        yield
        new_state = []
        for p in pairs:
            lhs = jnp.concatenate([u16[2 * p], u16[2 * p + 1], sub["vv2"][p]], axis=0)
            rhs = jnp.concatenate([jnp.where(own[0], sub["bt2"][p], zero16),
                                   jnp.where(own[1], sub["bt2"][p], zero16), sub["kt2"][p]], axis=0)
            upd = _dot_tn(lhs, rhs)
            upd = jnp.where(even_lanes, upd[:HEAD], upd[HEAD:])
            new_state.append((state[p] + upd) * sub["p_last"][:, cols[p]])
        carry["state"] = new_state
        yield
        outs = [x0[i][C:] + _dot(sub["lo"][i], jnp.concatenate([u16[i], sub["vv2"][p]], axis=0))
                for i, (p, q) in enumerate(hp)]
        for p in pairs:
            o_ref[0, rows, cols[p]] = jnp.where(even_lanes, outs[2 * p], outs[2 * p + 1])

    def run_interleaved(*gens):
        live = list(gens)
        while live:
            for gen in list(live):
                try:
                    next(gen)
                except StopIteration:
                    live.remove(gen)

    carry = dict(state=[s_scr[p] for p in pairs])
    subs = [dict() for _ in range(n_sub)]
    run_interleaved(*[phase_a(ci, subs[ci]) for ci in range(n_sub)])
    for ci in range(n_sub):
        run_interleaved(phase_b(ci, subs[ci], carry))
    for p in pairs:
        s_scr[p] = carry["state"][p]

    @pl.when(c == pl.num_programs(1) - 1)
    def _():
        s_out_ref[0] = s_scr[...]


def _wkv_chunked(r, k, v, lw, kk, bb, s0):
    B, T, D = r.shape
    heads = D // HEAD
    chunk = WKV_CHUNK
    assert chunk == HEAD and heads % 2 == 0
    Tp = -(-T // chunk) * chunk
    if Tp != T:
        r, k, v, lw, kk, bb = (jnp.pad(a, ((0, 0), (0, Tp - T), (0, 0))) for a in (r, k, v, lw, kk, bb))
    pair = lambda s: s.reshape(B, heads // 2, 2, HEAD, HEAD).transpose(0, 1, 3, 2, 4).reshape(
        B, heads // 2, HEAD, 2 * HEAD)
    unpair = lambda s: s.reshape(B, heads // 2, HEAD, 2, HEAD).transpose(0, 1, 3, 2, 4).reshape(
        B, heads, HEAD, HEAD)
    rows = chunk * WKV_SUB if Tp % (chunk * WKV_SUB) == 0 else chunk
    blk = pl.BlockSpec((1, rows, D), lambda b, c: (b, c, 0))
    st = pl.BlockSpec((1, heads // 2, HEAD, 2 * HEAD), lambda b, c: (b, 0, 0, 0))
    o, s_out = pl.pallas_call(
        functools.partial(_wkv_chunk_kernel, heads=heads,
                          n_stage=(min(T, chunk) - 1).bit_length() if Tp == chunk else chunk.bit_length() - 1),
        grid=(B, Tp // rows),
        in_specs=[blk] * 6 + [st],
        out_specs=[blk, st],
        out_shape=[jax.ShapeDtypeStruct((B, Tp, D), F32),
                   jax.ShapeDtypeStruct((B, heads // 2, HEAD, 2 * HEAD), F32)],
        scratch_shapes=[pltpu.VMEM((heads // 2, HEAD, 2 * HEAD), F32)],
        compiler_params=pltpu.CompilerParams(
            dimension_semantics=("parallel", "arbitrary"), vmem_limit_bytes=VMEM_LIMIT),
        name="wkv_chunk",
    )(r, k, v, lw, kk, bb, pair(s0))
    return o[:, :T], unpair(s_out)


def _qkv_kernel(h_ref, gn_ref, w_ref, q_out, kp_out, vp_out, kl_out, vl_out, *, npad, last_from):
    i = pl.program_id(1)
    D = h_ref.shape[2]

    @pl.when(i < npad)
    def _():
        kp_out[0] = jnp.zeros(kp_out.shape[1:], kp_out.dtype)
        vp_out[0] = jnp.zeros(vp_out.shape[1:], vp_out.dtype)

    @pl.when(i >= npad)
    def _():
        xn = _rms(h_ref[0], gn_ref[...]).astype(BF16)
        q = _dot(xn, w_ref[:, 0:D])
        k = _dot(xn, w_ref[:, D:2 * D])
        v = _dot(xn, w_ref[:, 2 * D:3 * D])
        q_out[0] = (q * (HEAD ** -0.5 * LOG2E)).astype(BF16)
        kp_out[0] = k.astype(BF16)
        vp_out[0] = v.astype(BF16)

        @pl.when(i >= npad + last_from)
        def _():
            kl_out[0] = k
            vl_out[0] = v


def _qkv_proj(h, gn, w_qkv, *, tq, pad_rows, keep):
    B, T, D = h.shape
    assert T % tq == 0 and pad_rows % tq == 0
    npad = pad_rows // tq
    assert keep % tq == 0
    last_from = (T - keep) // tq
    blk_in = pl.BlockSpec((1, tq, D), lambda b, i: (b, jnp.maximum(i - npad, 0), 0))
    blk_pad = pl.BlockSpec((1, tq, D), lambda b, i: (b, i, 0))
    blk_last = pl.BlockSpec((1, tq, D), lambda b, i: (b, jnp.maximum(i - npad - last_from, 0), 0))
    return pl.pallas_call(
        functools.partial(_qkv_kernel, npad=npad, last_from=last_from),
        grid=(B, npad + T // tq),
        in_specs=[blk_in, _const_spec(gn.shape), _const_spec(w_qkv.shape)],
        out_specs=[blk_in, blk_pad, blk_pad, blk_last, blk_last],
        out_shape=[jax.ShapeDtypeStruct((B, T, D), BF16),
                   jax.ShapeDtypeStruct((B, pad_rows + T, D), BF16),
                   jax.ShapeDtypeStruct((B, pad_rows + T, D), BF16),
                   jax.ShapeDtypeStruct((B, keep, D), F32),
                   jax.ShapeDtypeStruct((B, keep, D), F32)],
        compiler_params=pltpu.CompilerParams(
            dimension_semantics=("parallel", "arbitrary"), vmem_limit_bytes=VMEM_LIMIT),
        name="qkv_proj",
    )(h, gn, w_qkv)


def _toeplitz(line, rows):
    return pltpu.roll(jnp.broadcast_to(line, (rows, line.shape[1])), 0, 1, stride=1, stride_axis=0)


def _band_attn_kernel(q_ref, k_ref, v_ref, line_ref, o_ref, tbl_ref, *, nk, pad_rows):
    mq = q_ref.shape[1]
    W2 = 2 * HEAD
    n_cols = q_ref.shape[2] // W2
    c = pl.program_id(2)

    @pl.when((pl.program_id(1) == 0) & (c == 0))
    def _():
        qc = lax.broadcasted_iota(jnp.int32, (mq, nk), 0) // CHUNK
        kc = lax.broadcasted_iota(jnp.int32, (mq, nk), 1) // CHUNK
        in_band = (kc >= qc) & (kc <= qc + LEFT_CHUNKS)
        for i in range(tbl_ref.shape[0]):
            tbl_ref[i] = jnp.where(in_band, _toeplitz(line_ref[i], mq)[:, :nk], NEG_INF)

    start = pl.multiple_of(c * mq, mq)
    lane = lax.broadcasted_iota(jnp.int32, (mq, W2), 1)
    even_lanes = lane < HEAD
    own = [even_lanes, jnp.logical_not(even_lanes)]
    cols = [slice(p * W2, (p + 1) * W2) for p in range(n_cols)]
    hp = [(p, q) for p in range(n_cols) for q in range(2)]

    def attend(band_has_padding):
        kb = [k_ref[0, pl.ds(start, nk), cl] for cl in cols]
        vb = [v_ref[0, pl.ds(start, nk), cl] for cl in cols]
        q2 = [q_ref[0, :, cl] for cl in cols]
        s = [_dot_nt(jnp.where(own[q], q2[p], jnp.zeros_like(q2[p])), kb[p]) + tbl_ref[i]
             for i, (p, q) in enumerate(hp)]
        if band_has_padding:
            col = lax.broadcasted_iota(jnp.int32, (1, nk), 1)
            pen = jnp.where(col >= pad_rows - c * mq, 0.0, NEG_INF)
            s = [si + pen for si in s]
        m = [jnp.max(si, axis=-1, keepdims=True) for si in s]
        pr = [jnp.exp2(si - mi) for si, mi in zip(s, m)]
        l = [jnp.sum(pi, axis=-1, keepdims=True) for pi in pr]
        o = [_dot(pr[i].astype(BF16), vb[p]) * (1.0 / l[i]) for i, (p, q) in enumerate(hp)]
        for p in range(n_cols):
            o_ref[0, :, cols[p]] = jnp.where(even_lanes, o[2 * p], o[2 * p + 1]).astype(o_ref.dtype)

    pl.when(c * mq < pad_rows)(lambda: attend(True))
    pl.when(c * mq >= pad_rows)(lambda: attend(False))


def _band_attention(q, kpad, vpad, lines, *, mq, nk, pad_rows, heads_per_step):
    B, T, D = q.shape
    Tk = kpad.shape[1]
    width = heads_per_step * HEAD
    assert T % mq == 0 and D % width == 0 and width % (2 * HEAD) == 0
    q_spec = pl.BlockSpec((1, mq, width), lambda g, b, c: (b, c, g))
    kv_spec = pl.BlockSpec((1, Tk, width), lambda g, b, c: (b, 0, g))
    line_spec = pl.BlockSpec((heads_per_step, 1, lines.shape[2]), lambda g, b, c: (g, 0, 0))
    return pl.pallas_call(
        functools.partial(_band_attn_kernel, nk=nk, pad_rows=pad_rows),
        grid=(D // width, B, T // mq),
        in_specs=[q_spec, kv_spec, kv_spec, line_spec],
        out_specs=q_spec,
        out_shape=jax.ShapeDtypeStruct((B, T, D), BF16),
        scratch_shapes=[pltpu.VMEM((heads_per_step, mq, nk), F32)],
        compiler_params=pltpu.CompilerParams(
            dimension_semantics=("arbitrary", "arbitrary", "arbitrary"), vmem_limit_bytes=VMEM_LIMIT),
        name="band_attn",
    )(q, kpad, vpad, lines)


def _sample_attn_kernel(q_ref, kc_ref, vc_ref, kn_ref, vn_ref, line_ref, o_ref, tc_ref, tn_ref, *, heads):
    W2 = 2 * HEAD
    tq = q_ref.shape[1]
    nc = kc_ref.shape[1]

    @pl.when(pl.program_id(0) == 0)
    def _():
        for i in range(heads):
            t = _toeplitz(line_ref[i], tq)
            tc_ref[i] = t[:, :nc]
            tn_ref[i] = t[:, nc:nc + tq]

    lane = lax.broadcasted_iota(jnp.int32, (tq, W2), 1)
    even_lanes = lane < HEAD
    own = [even_lanes, jnp.logical_not(even_lanes)]
    hp = [(p, q) for p in range(heads // 2) for q in range(2)]
    cols = [slice(p * W2, (p + 1) * W2) for p in range(heads // 2)]
    q2 = [q_ref[0, :, cl] for cl in cols]
    qm = [jnp.where(own[q], q2[p], jnp.zeros_like(q2[p])) for p, q in hp]
    s_c = [_dot_nt(qm[i], kc_ref[0, :, cols[p]]) + tc_ref[i] for i, (p, q) in enumerate(hp)]
    s_n = [_dot_nt(qm[i], kn_ref[0, :, cols[p]]) + tn_ref[i] for i, (p, q) in enumerate(hp)]
    m = [jnp.maximum(jnp.max(a, axis=-1, keepdims=True), jnp.max(b, axis=-1, keepdims=True))
         for a, b in zip(s_c, s_n)]
    p_c = [jnp.exp2(a - mi) for a, mi in zip(s_c, m)]
    p_n = [jnp.exp2(b - mi) for b, mi in zip(s_n, m)]
    l = [jnp.sum(a, axis=-1, keepdims=True) + jnp.sum(b, axis=-1, keepdims=True) for a, b in zip(p_c, p_n)]
    o = [(_dot(p_c[i].astype(BF16), vc_ref[0, :, cols[p]]) + _dot(p_n[i].astype(BF16), vn_ref[0, :, cols[p]]))
         * (1.0 / l[i]) for i, (p, q) in enumerate(hp)]
    for p in range(heads // 2):
        o_ref[0, :, cols[p]] = jnp.where(even_lanes, o[2 * p], o[2 * p + 1]).astype(o_ref.dtype)


def _sample_attention(q, kc, vc, kn, vn, lines):
    B, T, D = q.shape
    W = kc.shape[1]
    heads = D // HEAD
    new = pl.BlockSpec((1, T, D), lambda b: (b, 0, 0))
    old = pl.BlockSpec((1, W, D), lambda b: (b, 0, 0))
    return pl.pallas_call(
        functools.partial(_sample_attn_kernel, heads=heads),
        grid=(B,),
        in_specs=[new, old, old, new, new, _const_spec(lines.shape)],
        out_specs=new,
        out_shape=jax.ShapeDtypeStruct((B, T, D), BF16),
        scratch_shapes=[pltpu.VMEM((heads, T, W), F32), pltpu.VMEM((heads, T, T), F32)],
        compiler_params=pltpu.CompilerParams(dimension_semantics=("arbitrary",), vmem_limit_bytes=VMEM_LIMIT),
        name="sample_attn",
    )(q, kc, vc, kn, vn, lines)


def _bias_lines(rel_bias, mq, nk):
    period = -(-(nk + mq) // 128) * 128
    e = jnp.arange(period, dtype=jnp.int32)
    e = jnp.where(e < nk, e, e - period)
    rel = jnp.clip(ATT_WINDOW - e, -(CHUNK - 1), REL_MAX_PAST) + (CHUNK - 1)
    return (rel_bias.astype(F32)[:, rel] * LOG2E)[:, None, :]


def _mix_ffn_kernel(*refs, rwkv, final, n_ff_chunks):
    it = iter(refs)
    h_ref = next(it)
    if rwkv:
        o_ref, bonus_ref, g_ref = next(it), next(it), next(it)
        lnw_ref, lnb_ref, seg_ref, exp_ref = next(it), next(it), next(it), next(it)
    else:
        a_ref = next(it)
    wo_ref, gf_ref, wgu_ref, wd_ref = next(it), next(it), next(it), next(it)
    gfin_ref = next(it) if final else None
    out_ref = next(it)

    if rwkv:
        o = o_ref[...]
        mean = _seg_expand(_seg_sum(o, seg_ref) * (1.0 / HEAD), exp_ref)
        d = o - mean
        rstd = lax.rsqrt(_seg_sum(d * d, seg_ref) * (1.0 / HEAD) + LNX_EPS)
        y = d * _seg_expand(rstd, exp_ref) * lnw_ref[...] + lnb_ref[...]
        a = ((y + bonus_ref[...]) * g_ref[...]).astype(BF16)
    else:
        a = a_ref[...]
    h1 = h_ref[...] + _dot(a, wo_ref[...])

    xn = _rms(h1, gf_ref[...]).astype(BF16)
    dff = wd_ref.shape[0]
    fc = dff // n_ff_chunks
    acc = h1
    for c in range(n_ff_chunks):
        gate = _dot(xn, wgu_ref[:, c * fc:(c + 1) * fc])
        up = _dot(xn, wgu_ref[:, dff + c * fc:dff + (c + 1) * fc])
        act = (gate * jax.nn.sigmoid(gate) * up).astype(BF16)
        acc = acc + _dot(act, wd_ref[c * fc:(c + 1) * fc, :])
    if final:
        acc = _rms(acc, gfin_ref[...])
    out_ref[...] = acc


def _mix_ffn(h, mix_inputs, mix_weights, w_out, gf, w_gu, w_down, g_final, *, rwkv, tm, n_ff_chunks):
    N, D = h.shape
    assert N % tm == 0
    final = g_final is not None
    row = pl.BlockSpec((tm, D), lambda i: (i, 0))
    consts = list(mix_weights) + [w_out, gf, w_gu, w_down] + ([g_final] if final else [])
    return pl.pallas_call(
        functools.partial(_mix_ffn_kernel, rwkv=rwkv, final=final, n_ff_chunks=n_ff_chunks),
        grid=(N // tm,),
        in_specs=[row] + [row] * len(mix_inputs) + [_const_spec(w.shape) for w in consts],
        out_specs=row,
        out_shape=jax.ShapeDtypeStruct((N, D), F32),
        compiler_params=pltpu.CompilerParams(
            dimension_semantics=("parallel",), vmem_limit_bytes=VMEM_LIMIT),
        name="mix_ffn_rwkv" if rwkv else "mix_ffn_attn",
    )(h, *mix_inputs, *consts)


def _row_block(n, want):
    t = min(want, n)
    while n % t:
        t //= 2
    return t


def kernel(x_prompt, x_sample, state_wkv, state_shift, cache_k, cache_v, norm_mix, norm_ffn, norm_final, rwkv_mu, rwkv_w_rkv, rwkv_w_out, rwkv_decay_w0, rwkv_decay_w1, rwkv_decay_w2, rwkv_iclr_a0, rwkv_iclr_a1, rwkv_iclr_a2, rwkv_vres_v0, rwkv_vres_v1, rwkv_vres_v2, rwkv_gate_g1, rwkv_gate_g2, rwkv_k_k, rwkv_k_a, rwkv_r_k, rwkv_lnx_w, rwkv_lnx_b, attn_w_qkv, attn_w_out, attn_rel_bias, ffn_w_gu, ffn_w_down):
    B, T, D = x_prompt.shape
    Bs, Ts, _ = x_sample.shape
    depth = norm_mix.shape[0]
    heads = D // HEAD
    W = cache_k.shape[2]
    assert W == ATT_WINDOW and Ts <= CHUNK and PAST_LEN % CHUNK == 0

    vec = lambda a: a.reshape(1, D).astype(F32)
    bf = lambda a: a.astype(BF16)
    split_heads = lambda xs, b: jnp.stack(xs).reshape(len(xs), b, -1, heads, HEAD)

    col = jnp.arange(D, dtype=jnp.int32)[:, None] // HEAD
    seg = (col == jnp.arange(SEG_LANES, dtype=jnp.int32)[None, :]).astype(BF16)
    exp = jnp.concatenate([seg.T, seg.T], axis=0)

    hp, hs = x_prompt, x_sample
    vf_p = vf_s = None
    wkv_p, shift_p, k_p, v_p = [], [], [], []
    wkv_s, shift_s, k_s, v_s = [], [], [], []

    tm_proj = _row_block(T, 256)
    tm_ffn_p = _row_block(B * T, 512)
    tm_ffn_pa = _row_block(B * T, 512)
    tm_ffn_s = _row_block(Bs * Ts, 512)
    tq = _row_block(min(T, ATT_WINDOW), 512)
    group = _row_block(T, 4 * CHUNK)
    nk_p = ATT_WINDOW + group

    for layer in range(depth):
        j = layer // 2
        gn = vec(norm_mix[layer])
        last = layer == depth - 1
        ffn_args = dict(gf=vec(norm_ffn[layer]), w_gu=bf(ffn_w_gu[layer]), w_down=bf(ffn_w_down[layer]),
                        g_final=vec(norm_final) if last else None, n_ff_chunks=1)
        if layer % 2 == 0:
            p = dict(mu=jnp.pad(rwkv_mu[j].astype(F32), ((0, 2), (0, 0))), w_rkv=bf(rwkv_w_rkv[j]),
                     w0=vec(rwkv_decay_w0[j]), w1=bf(rwkv_decay_w1[j]), w2=bf(rwkv_decay_w2[j]),
                     a0=vec(rwkv_iclr_a0[j]), a1=bf(rwkv_iclr_a1[j]), a2=bf(rwkv_iclr_a2[j]),
                     g1=bf(rwkv_gate_g1[j]), g2=bf(rwkv_gate_g2[j]),
                     k_k=vec(rwkv_k_k[j]), k_a=vec(rwkv_k_a[j]), r_k=vec(rwkv_r_k[j]))
            if j > 0:
                p.update(v0=vec(rwkv_vres_v0[j - 1]), v1=bf(rwkv_vres_v1[j - 1]), v2=bf(rwkv_vres_v2[j - 1]))
            lnw, lnb, w_out = vec(rwkv_lnx_w[j]), vec(rwkv_lnx_b[j]), bf(rwkv_w_out[j])

            def rwkv_side(h, shift, s0, vfirst, tm_proj, tm_ffn):
                b_, t_, _ = h.shape
                (r, k, v, lw, kk, bb, g, bonus), x_last = _tmix_proj(
                    h, shift, vfirst, gn, p, seg, exp, tm=tm_proj)
                o, s_new = _wkv_chunked(r, k, v, lw, kk, bb, s0)
                rows = lambda a: a.reshape(b_ * t_, D)
                h_new = _mix_ffn(rows(h), (rows(o), rows(bonus), rows(g)), (lnw, lnb, seg, exp), w_out,
                                 rwkv=True, tm=tm_ffn, **ffn_args)
                return h_new.reshape(h.shape), x_last, s_new, (v if vfirst is None else vfirst)

            hp, sh, st, vf_p = rwkv_side(hp, jnp.zeros((B, D), F32), jnp.zeros((B, heads, HEAD, HEAD), F32),
                                         vf_p, tm_proj, tm_ffn_p)
            wkv_p.append(st); shift_p.append(sh)
            hs, sh, st, vf_s = rwkv_side(hs, state_shift[j], state_wkv[j].astype(F32),
                                         vf_s, tm_ffn_s, tm_ffn_s)
            wkv_s.append(st); shift_s.append(sh)
        else:
            w_qkv, w_out = bf(attn_w_qkv[j]), bf(attn_w_out[j])
            q, kpad, vpad, k_last, v_last = _qkv_proj(hp, gn, w_qkv, tq=tq, pad_rows=ATT_WINDOW,
                                                      keep=min(ATT_WINDOW, T))
            att = _band_attention(q, kpad, vpad, _bias_lines(attn_rel_bias[j], group, nk_p), mq=group, nk=nk_p,
                                  pad_rows=ATT_WINDOW, heads_per_step=8)
            hp = _mix_ffn(hp.reshape(B * T, D), (att.reshape(B * T, D),), (), w_out,
                          rwkv=False, tm=tm_ffn_pa, **ffn_args).reshape(B, T, D)
            k_p.append(k_last); v_p.append(v_last)
            ns = Bs * Ts
            q, k16, v16, k_new, v_new = _qkv_proj(hs.reshape(1, ns, D), gn, w_qkv, tq=tm_ffn_s, pad_rows=0, keep=ns)
            att = _sample_attention(q.reshape(Bs, Ts, D), bf(cache_k[j]).reshape(Bs, W, D),
                                    bf(cache_v[j]).reshape(Bs, W, D), k16.reshape(Bs, Ts, D),
                                    v16.reshape(Bs, Ts, D), _bias_lines(attn_rel_bias[j], Ts, W + Ts))
            hs = _mix_ffn(hs.reshape(ns, D), (att.reshape(ns, D),), (), w_out,
                          rwkv=False, tm=tm_ffn_s, **ffn_args).reshape(Bs, Ts, D)
            k_s.append(k_new); v_s.append(v_new)

    return (hp, hs,
            jnp.stack(wkv_p), jnp.stack(shift_p), split_heads(k_p, B), split_heads(v_p, B),
            jnp.stack(wkv_s), jnp.stack(shift_s), split_heads(k_s, Bs), split_heads(v_s, Bs))
```
